```python
import math
import jax, jax.numpy as jnp
from jax import lax
import numpy as np

D_MODEL = 4096
BATCH = 1
SEQ = 8192
DEPTH = 1
DEC_BATCH = 128
DEC_SEQ = 1
PAST_LEN = 2048
PAGE_SIZE = 128

A_WIDTH = D_MODEL // 2
A_GROUP_WIDTH = 128
A_GROUPS = A_WIDTH // A_GROUP_WIDTH
CHUNK = 128
HEAD_DIM = 128
HEAD_WIDTH = 2 * HEAD_DIM
N_HEADS = D_MODEL // (4 * HEAD_DIM)
B_WIDTH = N_HEADS * HEAD_WIDTH
Q_BLOCK = 128
D_FF = ((8 * D_MODEL // 3 + 255) // 256) * 256
CONV_W = 3
N_IN = 2 * A_WIDTH + 3 * B_WIDTH + 2 * D_MODEL
EPS = 1e-6
NEG_INF = -1e30

kernel_name = "hybrid_gmlp_diffattn_convffn_step"


def rmsnorm(x, g):
    x32 = x.astype(jnp.float32)
    y = x32 * lax.rsqrt(jnp.mean(x32 * x32, axis=-1, keepdims=True) + EPS)
    return (y * g.astype(jnp.float32)).astype(x.dtype)


def layernorm(x, g, b):
    x32 = x.astype(jnp.float32)
    mu = jnp.mean(x32, axis=-1, keepdims=True)
    xc = x32 - mu
    var = jnp.mean(xc * xc, axis=-1, keepdims=True)
    y = xc * lax.rsqrt(var + EPS) * g.astype(jnp.float32) + b.astype(jnp.float32)
    return y.astype(x.dtype)


def alibi_slopes():
    return jnp.asarray(2.0 ** (-8.0 * np.arange(1, N_HEADS + 1, dtype=np.float32) / N_HEADS), dtype=jnp.float32)


def in_projection(x, g_pre, w_in):
    h = rmsnorm(x, g_pre)
    z = h @ w_in
    cuts = np.cumsum([A_WIDTH, A_WIDTH, B_WIDTH, B_WIDTH, B_WIDTH, D_MODEL]).tolist()
    u, va, q, k, v, ga, gb = jnp.split(z, cuts, axis=-1)
    bsz, L = x.shape[:2]
    hs = (bsz, L, N_HEADS, HEAD_WIDTH)
    return u, va, q.reshape(hs), k.reshape(hs), v.reshape(hs), ga, gb


def chunk_spatial_gate(u, v, w_s, b_s):
    bsz, L, _ = v.shape
    n_chunks = -(-L // CHUNK)
    pad = n_chunks * CHUNK - L
    vp = jnp.pad(v, ((0, 0), (0, pad), (0, 0))).reshape(bsz, n_chunks, CHUNK, A_GROUPS, A_GROUP_WIDTH)
    causal = jnp.tril(jnp.ones((CHUNK, CHUNK), dtype=w_s.dtype))
    mixed = jnp.einsum('gij,bcjgd->bcigd', w_s * causal, vp) + jnp.transpose(b_s)[None, None, :, :, None]
    mixed = mixed.reshape(bsz, n_chunks * CHUNK, A_WIDTH)[:, :L]
    return u * mixed.astype(u.dtype)


def mixer_a(u, va, ln_g, ln_b, w_s, b_s):
    u = jax.nn.gelu(u, approximate=True)
    vn = layernorm(jax.nn.gelu(va, approximate=True), ln_g, ln_b)
    return chunk_spatial_gate(u, vn, w_s, b_s), vn


def diff_attention(q, k, v, q_pos, k_pos, lam, slopes):
    f32 = jnp.float32
    q = q.astype(f32) * (HEAD_DIM ** -0.5)
    k = k.astype(f32)
    dist = (q_pos[:, None] - k_pos[None, :]).astype(f32)
    bias = (-slopes[:, None, None] * dist[None])[None]
    visible = (dist >= 0)[None, None]

    def probs(qh, kh):
        s = jnp.einsum('bqhd,bkhd->bhqk', qh, kh) + bias
        return jax.nn.softmax(jnp.where(visible, s, NEG_INF), axis=-1)

    p = probs(q[..., :HEAD_DIM], k[..., :HEAD_DIM]) - lam * probs(q[..., HEAD_DIM:], k[..., HEAD_DIM:])
    return jnp.einsum('bhqk,bkhd->bqhd', p, v.astype(f32))


def prompt_diff_attention(q, k, v, lam, slopes):
    bsz, L = q.shape[:2]
    nb = L // Q_BLOCK
    qb = q.reshape(bsz, nb, Q_BLOCK, N_HEADS, HEAD_WIDTH).transpose(1, 0, 2, 3, 4)
    k_pos = jnp.arange(L)

    def block(args):
        i, qi = args
        q_pos = i * Q_BLOCK + jnp.arange(Q_BLOCK)
        return diff_attention(qi, k, v, q_pos, k_pos, lam, slopes)

    out = lax.map(block, (jnp.arange(nb), qb))
    return out.transpose(1, 0, 2, 3, 4).reshape(bsz, L, N_HEADS, HEAD_WIDTH)


def sample_diff_attention(q, k_new, v_new, pool_k, pool_v, page_table, lam, slopes):
    nb, npg = page_table.shape
    past = npg * PAGE_SIZE
    L = q.shape[1]
    k_past = pool_k[page_table].reshape(nb, past, N_HEADS, HEAD_WIDTH)
    v_past = pool_v[page_table].reshape(nb, past, N_HEADS, HEAD_WIDTH)
    k_all = jnp.concatenate([k_past, k_new.astype(k_past.dtype)], axis=1)
    v_all = jnp.concatenate([v_past, v_new.astype(v_past.dtype)], axis=1)
    q_pos = past + jnp.arange(L)
    k_pos = jnp.arange(past + L)
    return diff_attention(q, k_all, v_all, q_pos, k_pos, lam, slopes)


def merge_branches(x, ya, att, ga, gb, subln_g, lam_init, w_branch_a, w_branch_b, w_out, g_post):
    heads = rmsnorm(att, subln_g) * (1.0 - lam_init)
    yb = heads.reshape(heads.shape[0], heads.shape[1], B_WIDTH).astype(x.dtype)
    mixed = jax.nn.sigmoid(ga) * (ya @ w_branch_a) + jax.nn.sigmoid(gb) * (yb @ w_branch_b)
    return x + rmsnorm(mixed @ w_out, g_post)


def conv_ffn(x, conv_prev, g_pre, w_up, conv_w, conv_b, w_down, g_post):
    h = rmsnorm(x, g_pre)
    up = h @ w_up
    L = up.shape[1]
    ext = jnp.concatenate([conv_prev.astype(up.dtype), up], axis=1)
    conv = conv_b + conv_w[0] * ext[:, 0:L]
    for i in range(1, CONV_W):
        conv = conv + conv_w[i] * ext[:, i:i + L]
    gate, val = jnp.split(conv, 2, axis=-1)
    out = (jax.nn.gelu(gate, approximate=True) * val) @ w_down
    return x + rmsnorm(out, g_post), ext[:, L:]


def setup_inputs(seed: int = 0) -> dict:
    key = jax.random.key(seed)
    ks = jax.random.split(key, 32)
    f32 = jnp.float32
    n_pages = PAST_LEN // PAGE_SIZE
    n_used = DEC_BATCH * n_pages
    n_pool = n_used + max(1, n_used // 4)

    def nrm(k, shape, scale):
        return jax.random.normal(k, shape, f32) * scale

    page_table = jax.random.permutation(ks[0], n_pool)[:n_used].reshape(DEC_BATCH, n_pages).astype(jnp.int32)
    return {
        "x_prompt": nrm(ks[1], (BATCH, SEQ, D_MODEL), 1.0),
        "x_sample": nrm(ks[2], (DEC_BATCH, DEC_SEQ, D_MODEL), 1.0),
        "cache_k": nrm(ks[3], (DEPTH, n_pool, PAGE_SIZE, N_HEADS, HEAD_WIDTH), 1.0),
        "cache_v": nrm(ks[4], (DEPTH, n_pool, PAGE_SIZE, N_HEADS, HEAD_WIDTH), 1.0),
        "state_ffn_conv": nrm(ks[5], (DEPTH, DEC_BATCH, CONV_W - 1, 2 * D_FF), 1.0),
        "page_table": page_table,
        "g_pre_mix": 1.0 + nrm(ks[6], (DEPTH, D_MODEL), 0.02),
        "w_in": nrm(ks[7], (DEPTH, D_MODEL, N_IN), D_MODEL ** -0.5),
        "a_ln_g": 1.0 + nrm(ks[8], (DEPTH, A_WIDTH), 0.02),
        "a_ln_b": nrm(ks[9], (DEPTH, A_WIDTH), 0.02),
        "a_w_s": nrm(ks[10], (DEPTH, A_GROUPS, CHUNK, CHUNK), CHUNK ** -0.5),
        "a_b_s": 1.0 + nrm(ks[11], (DEPTH, A_GROUPS, CHUNK), 0.02),
        "lambda_q1": nrm(ks[12], (DEPTH, HEAD_DIM), 0.1),
        "lambda_k1": nrm(ks[13], (DEPTH, HEAD_DIM), 0.1),
        "lambda_q2": nrm(ks[14], (DEPTH, HEAD_DIM), 0.1),
        "lambda_k2": nrm(ks[15], (DEPTH, HEAD_DIM), 0.1),
        "subln_g": 1.0 + nrm(ks[16], (DEPTH, HEAD_WIDTH), 0.02),
        "w_branch_a": nrm(ks[17], (DEPTH, A_WIDTH, D_MODEL), A_WIDTH ** -0.5),
        "w_branch_b": nrm(ks[18], (DEPTH, B_WIDTH, D_MODEL), B_WIDTH ** -0.5),
        "w_out": nrm(ks[19], (DEPTH, D_MODEL, D_MODEL), D_MODEL ** -0.5),
        "g_post_mix": 1.0 + nrm(ks[20], (DEPTH, D_MODEL), 0.02),
        "g_pre_ffn": 1.0 + nrm(ks[21], (DEPTH, D_MODEL), 0.02),
        "w_up": nrm(ks[22], (DEPTH, D_MODEL, 2 * D_FF), D_MODEL ** -0.5),
        "conv_w": nrm(ks[23], (DEPTH, CONV_W, 2 * D_FF), CONV_W ** -0.5),
        "conv_b": nrm(ks[24], (DEPTH, 2 * D_FF), 0.02),
        "w_down": nrm(ks[25], (DEPTH, D_FF, D_MODEL), D_FF ** -0.5),
        "g_post_ffn": 1.0 + nrm(ks[26], (DEPTH, D_MODEL), 0.02),
    }


def reference(x_prompt, x_sample, cache_k, cache_v, state_ffn_conv, page_table,
              g_pre_mix, w_in, a_ln_g, a_ln_b, a_w_s, a_b_s,
              lambda_q1, lambda_k1, lambda_q2, lambda_k2, subln_g,
              w_branch_a, w_branch_b, w_out, g_post_mix,
              g_pre_ffn, w_up, conv_w, conv_b, w_down, g_post_ffn):
    f32 = jnp.float32
    slopes = alibi_slopes()
    xp, xs = x_prompt, x_sample
    kp_l, vp_l, ks_l, vs_l, av_l, cp_l, cs_l = [], [], [], [], [], [], []
    for l in range(DEPTH):
        lam_init = 0.8 - 0.6 * math.exp(-0.3 * l)
        lam = (jnp.exp(jnp.sum(lambda_q1[l].astype(f32) * lambda_k1[l].astype(f32)))
               - jnp.exp(jnp.sum(lambda_q2[l].astype(f32) * lambda_k2[l].astype(f32))) + lam_init)

        u, va, q, k, v, ga, gb = in_projection(xp, g_pre_mix[l], w_in[l])
        ya, _ = mixer_a(u, va, a_ln_g[l], a_ln_b[l], a_w_s[l], a_b_s[l])
        att = prompt_diff_attention(q, k, v, lam, slopes)
        xp = merge_branches(xp, ya, att, ga, gb, subln_g[l], lam_init,
                            w_branch_a[l], w_branch_b[l], w_out[l], g_post_mix[l])
        conv_zero = jnp.zeros((xp.shape[0], CONV_W - 1, 2 * D_FF), dtype=xp.dtype)
        xp, conv_p = conv_ffn(xp, conv_zero, g_pre_ffn[l], w_up[l], conv_w[l], conv_b[l], w_down[l], g_post_ffn[l])
        kp_l.append(k)
        vp_l.append(v)
        cp_l.append(conv_p)

        u, va, q, k, v, ga, gb = in_projection(xs, g_pre_mix[l], w_in[l])
        ya, vn = mixer_a(u, va, a_ln_g[l], a_ln_b[l], a_w_s[l], a_b_s[l])
        att = sample_diff_attention(q, k, v, cache_k[l], cache_v[l], page_table, lam, slopes)
        xs = merge_branches(xs, ya, att, ga, gb, subln_g[l], lam_init,
                            w_branch_a[l], w_branch_b[l], w_out[l], g_post_mix[l])
        xs, conv_s = conv_ffn(xs, state_ffn_conv[l], g_pre_ffn[l], w_up[l], conv_w[l], conv_b[l], w_down[l], g_post_ffn[l])
        ks_l.append(k)
        vs_l.append(v)
        av_l.append(vn)
        cs_l.append(conv_s)

    new_k_prompt = jnp.stack(kp_l, axis=0)
    new_v_prompt = jnp.stack(vp_l, axis=0)
    new_k_sample = jnp.stack(ks_l, axis=0)
    new_v_sample = jnp.stack(vs_l, axis=0)
    new_chunk_v_sample = jnp.stack(av_l, axis=0)
    new_conv_prompt = jnp.stack(cp_l, axis=0)
    new_conv_sample = jnp.stack(cs_l, axis=0)
    return (xp, xs, new_k_prompt, new_v_prompt, new_k_sample, new_v_sample,
            new_chunk_v_sample, new_conv_prompt, new_conv_sample)
```

```python
import functools
import math

import numpy as np
import jax
import jax.numpy as jnp
from jax import lax
from jax.experimental import pallas as pl
from jax.experimental.pallas import tpu as pltpu

D_MODEL = 4096
SEQ = 8192
DEC_BATCH = 128
PAST_LEN = 2048
PAGE_SIZE = 128
N_PAGES = PAST_LEN // PAGE_SIZE
A_WIDTH = D_MODEL // 2
A_GROUP_WIDTH = 128
A_GROUPS = A_WIDTH // A_GROUP_WIDTH
CHUNK = 128
HEAD_DIM = 128
HEAD_WIDTH = 2 * HEAD_DIM
N_HEADS = D_MODEL // (4 * HEAD_DIM)
B_WIDTH = N_HEADS * HEAD_WIDTH
D_FF = ((8 * D_MODEL // 3 + 255) // 256) * 256
CONV_W = 3
N_IN = 2 * A_WIDTH + 3 * B_WIDTH + 2 * D_MODEL
EPS = 1e-6
NEG_INF = -1e30
LAM_INIT = 0.8 - 0.6 * math.exp(-0.3 * 0)
Q_SCALE = HEAD_DIM ** -0.5

COL_U = 0
COL_VA = A_WIDTH
COL_Q = 2 * A_WIDTH
COL_K = COL_Q + B_WIDTH
COL_V = COL_K + B_WIDTH
COL_GA = COL_V + B_WIDTH
COL_GB = COL_GA + D_MODEL

V7X_VMEM_LIMIT_BYTES = 58 * 1024 * 1024
FFN_TN = 256
FFN_PANELS = D_FF // FFN_TN
ATT_T = 512
PAGES_PER_STEP = 8

BF16 = jnp.bfloat16
F32 = jnp.float32


def _params(*sem):
    return pltpu.CompilerParams(dimension_semantics=sem, vmem_limit_bytes=V7X_VMEM_LIMIT_BYTES)


def _gelu(x):
    return 0.5 * x * (1.0 + jnp.tanh(0.7978845608028654 * (x + 0.044715 * (x * x * x))))


def _rms(x, g):
    return x * lax.rsqrt(jnp.mean(x * x, axis=-1, keepdims=True) + EPS) * g


def _rms_cast_kernel(x_ref, g_ref, o_ref):
    o_ref[...] = _rms(x_ref[...], g_ref[...]).astype(o_ref.dtype)


def rms_cast(x, g, tm):
    m, d = x.shape
    return pl.pallas_call(
        _rms_cast_kernel, grid=(m // tm,),
        in_specs=[pl.BlockSpec((tm, d), lambda i: (i, 0)), pl.BlockSpec((1, d), lambda i: (0, 0))],
        out_specs=pl.BlockSpec((tm, d), lambda i: (i, 0)),
        out_shape=jax.ShapeDtypeStruct((m, d), BF16),
        compiler_params=_params("arbitrary"), name="rms_cast")(x, g.reshape(1, d))


def _post_mix_kernel(x_ref, t_ref, g1_ref, g2_ref, x1_ref, h_ref):
    x1 = x_ref[...] + _rms(t_ref[...], g1_ref[...])
    x1_ref[...] = x1
    h_ref[...] = _rms(x1, g2_ref[...]).astype(h_ref.dtype)


def post_mix(x, t, g_post, g_pre_next, tm):
    m, d = x.shape
    row = pl.BlockSpec((tm, d), lambda i: (i, 0))
    vec = pl.BlockSpec((1, d), lambda i: (0, 0))
    return pl.pallas_call(
        _post_mix_kernel, grid=(m // tm,), in_specs=[row, row, vec, vec], out_specs=[row, row],
        out_shape=[jax.ShapeDtypeStruct((m, d), F32), jax.ShapeDtypeStruct((m, d), BF16)],
        compiler_params=_params("arbitrary"), name="post_mix")(
            x, t, g_post.reshape(1, d), g_pre_next.reshape(1, d))


def _post_ffn_kernel(x_ref, t_ref, g_ref, y_ref):
    y_ref[...] = x_ref[...] + _rms(t_ref[...], g_ref[...])


def post_ffn(x, t, g, tm):
    m, d = x.shape
    row = pl.BlockSpec((tm, d), lambda i: (i, 0))
    vec = pl.BlockSpec((1, d), lambda i: (0, 0))
    return pl.pallas_call(
        _post_ffn_kernel, grid=(m // tm,), in_specs=[row, row, vec], out_specs=row,
        out_shape=jax.ShapeDtypeStruct((m, d), F32),
        compiler_params=_params("arbitrary"), name="post_ffn")(x, t, g.reshape(1, d))


def _mm_kernel(a_ref, w_ref, o_ref):
    o_ref[...] = jnp.dot(a_ref[...], w_ref[...].astype(BF16),
                         preferred_element_type=F32).astype(o_ref.dtype)


def matmul(a, w, tm, tn, name):
    m, k = a.shape
    n = w.shape[1]
    return pl.pallas_call(
        _mm_kernel, grid=(m // tm, n // tn),
        in_specs=[pl.BlockSpec((tm, k), lambda i, j: (i, 0), pipeline_mode=pl.Buffered(1)),
                  pl.BlockSpec((k, tn), lambda i, j: (0, j))],
        out_specs=pl.BlockSpec((tm, tn), lambda i, j: (i, j)),
        out_shape=jax.ShapeDtypeStruct((m, n), F32),
        compiler_params=_params("arbitrary", "arbitrary"), name=name)(a, w)


def _layernorm(x, g, b):
    mu = jnp.mean(x, axis=-1, keepdims=True)
    xc = x - mu
    var = jnp.mean(xc * xc, axis=-1, keepdims=True)
    return xc * lax.rsqrt(var + EPS) * g + b


def _branch_a_prompt_kernel(u_ref, va_ref, lng_ref, lnb_ref, ws_ref, bst_ref, ya_ref):
    u = _gelu(u_ref[...])
    vn = _layernorm(_gelu(va_ref[...]), lng_ref[...], lnb_ref[...])
    row = lax.broadcasted_iota(jnp.int32, (CHUNK, CHUNK), 0)
    col = lax.broadcasted_iota(jnp.int32, (CHUNK, CHUNK), 1)
    causal = row >= col
    for g in range(A_GROUPS):
        sl = slice(g * A_GROUP_WIDTH, (g + 1) * A_GROUP_WIDTH)
        w = jnp.where(causal, ws_ref[g], 0.0).astype(BF16)
        mixed = jnp.dot(w, vn[:, sl].astype(BF16), preferred_element_type=F32) + bst_ref[:, g:g + 1]
        ya_ref[:, sl] = (u[:, sl] * mixed).astype(ya_ref.dtype)


def branch_a_prompt(z, ln_g, ln_b, w_s, b_s):
    m = z.shape[0]
    vec = pl.BlockSpec((1, A_WIDTH), lambda i: (0, 0))
    return pl.pallas_call(
        _branch_a_prompt_kernel, grid=(m // CHUNK,),
        in_specs=[pl.BlockSpec((CHUNK, A_WIDTH), lambda i: (i, COL_U // A_WIDTH)),
                  pl.BlockSpec((CHUNK, A_WIDTH), lambda i: (i, COL_VA // A_WIDTH)),
                  vec, vec,
                  pl.BlockSpec((A_GROUPS, CHUNK, CHUNK), lambda i: (0, 0, 0)),
                  pl.BlockSpec((CHUNK, A_GROUPS), lambda i: (0, 0))],
        out_specs=pl.BlockSpec((CHUNK, A_WIDTH), lambda i: (i, 0)),
        out_shape=jax.ShapeDtypeStruct((m, A_WIDTH), BF16),
        compiler_params=_params("arbitrary"), name="branch_a_prompt")(
            z, z, ln_g.reshape(1, A_WIDTH), ln_b.reshape(1, A_WIDTH), w_s, b_s.T)


def _branch_a_sample_kernel(u_ref, va_ref, lng_ref, lnb_ref, w00_ref, b0_ref, ya_ref, vn_ref):
    u = _gelu(u_ref[...])
    vn = _layernorm(_gelu(va_ref[...]), lng_ref[...], lnb_ref[...])
    vn_ref[...] = vn
    ya_ref[...] = (u * (vn * w00_ref[...] + b0_ref[...])).astype(ya_ref.dtype)


def branch_a_sample(z, ln_g, ln_b, w_s, b_s):
    m = z.shape[0]
    w00 = jnp.repeat(w_s[:, 0, 0], A_GROUP_WIDTH).reshape(1, A_WIDTH)
    b0 = jnp.repeat(b_s[:, 0], A_GROUP_WIDTH).reshape(1, A_WIDTH)
    vec = pl.BlockSpec((1, A_WIDTH), lambda i: (0, 0))
    out = pl.BlockSpec((m, A_WIDTH), lambda i: (0, 0))
    return pl.pallas_call(
        _branch_a_sample_kernel, grid=(1,),
        in_specs=[pl.BlockSpec((m, A_WIDTH), lambda i: (0, COL_U // A_WIDTH)),
                  pl.BlockSpec((m, A_WIDTH), lambda i: (0, COL_VA // A_WIDTH)),
                  vec, vec, vec, vec],
        out_specs=[out, out],
        out_shape=[jax.ShapeDtypeStruct((m, A_WIDTH), BF16), jax.ShapeDtypeStruct((m, A_WIDTH), F32)],
        compiler_params=_params("arbitrary"), name="branch_a_sample")(
            z, z, ln_g.reshape(1, A_WIDTH), ln_b.reshape(1, A_WIDTH), w00, b0)


def _lambda(lq1_ref, lk1_ref, lq2_ref, lk2_ref):
    return (jnp.exp(jnp.sum(lq1_ref[...] * lk1_ref[...], axis=1, keepdims=True))
            - jnp.exp(jnp.sum(lq2_ref[...] * lk2_ref[...], axis=1, keepdims=True)) + LAM_INIT)


def _alibi_slopes():
    return np.asarray(2.0 ** (-8.0 * np.arange(1, N_HEADS + 1, dtype=np.float32) / N_HEADS), dtype=np.float32)


def _attn_prompt_kernel(qi_tab, kj_tab, slopes_ref,
                        q_ref, k_ref, v_ref, lq1_ref, lk1_ref, lq2_ref, lk2_ref, sg_ref,
                        o_ref,
                        qs_ref, relb_ref, m_ref, l_ref, acc_ref):
    t_tile = ATT_T
    h = pl.program_id(0)
    t = pl.program_id(1)
    qi = qi_tab[t]
    kj = kj_tab[t]
    slope = slopes_ref[h]

    @pl.when(t == 0)
    def _():
        ii = lax.broadcasted_iota(jnp.int32, (t_tile, t_tile), 0)
        jj = lax.broadcasted_iota(jnp.int32, (t_tile, t_tile), 1)
        rel = (-slope) * (ii - jj).astype(F32)
        relb_ref[0] = rel
        relb_ref[1] = jnp.where(jj > ii, NEG_INF, rel)

    @pl.when(kj == 0)
    def _():
        qs_ref[...] = (q_ref[...] * Q_SCALE).astype(BF16)
        m_ref[...] = jnp.full(m_ref.shape, NEG_INF, F32)
        l_ref[...] = jnp.zeros(l_ref.shape, F32)
        acc_ref[...] = jnp.zeros(acc_ref.shape, F32)

    k = k_ref[...].astype(BF16)
    v = v_ref[...].astype(BF16)
    relb = relb_ref[(kj == qi).astype(jnp.int32)]
    c = (-slope) * ((qi - kj) * t_tile).astype(F32)
    for half in range(2):
        sl = slice(half * HEAD_DIM, (half + 1) * HEAD_DIM)
        s = lax.dot_general(qs_ref[:, sl], k[:, sl], (((1,), (1,)), ((), ())),
                            preferred_element_type=F32) + relb
        m_prev = m_ref[half]
        m_next = jnp.maximum(m_prev, jnp.max(s, axis=1, keepdims=True) + c)
        alpha = jnp.exp(m_prev - m_next)
        p = jnp.exp(s - (m_next[:, :1] - c))
        l_ref[half] = alpha * l_ref[half] + jnp.sum(p, axis=1, keepdims=True)
        m_ref[half] = m_next
        acc_ref[half] = acc_ref[half] * alpha[:, :1] + jnp.dot(p.astype(BF16), v, preferred_element_type=F32)

    @pl.when(kj == qi)
    def _():
        lam = _lambda(lq1_ref, lk1_ref, lq2_ref, lk2_ref)
        out = acc_ref[0] / l_ref[0][:, :1] - lam * (acc_ref[1] / l_ref[1][:, :1])
        o_ref[...] = (_rms(out, sg_ref[...]) * (1.0 - LAM_INIT)).astype(o_ref.dtype)


def attn_prompt(z, lq1, lk1, lq2, lk2, subln_g):
    m = z.shape[0]
    t_tile = ATT_T
    nq = m // t_tile
    pairs = [(qi, kj) for qi in range(nq) for kj in range(qi + 1)]
    qi_tab = jnp.asarray([p[0] for p in pairs], jnp.int32)
    kj_tab = jnp.asarray([p[1] for p in pairs], jnp.int32)
    slopes = jnp.asarray(_alibi_slopes())
    qb, kb, vb = COL_Q // HEAD_WIDTH, COL_K // HEAD_WIDTH, COL_V // HEAD_WIDTH
    lvec = pl.BlockSpec((1, HEAD_DIM), lambda h, t, qt, kt, sl: (0, 0))
    grid_spec = pltpu.PrefetchScalarGridSpec(
        num_scalar_prefetch=3, grid=(N_HEADS, len(pairs)),
        in_specs=[pl.BlockSpec((t_tile, HEAD_WIDTH), lambda h, t, qt, kt, sl: (qt[t], qb + h)),
                  pl.BlockSpec((t_tile, HEAD_WIDTH), lambda h, t, qt, kt, sl: (kt[t], kb + h)),
                  pl.BlockSpec((t_tile, HEAD_WIDTH), lambda h, t, qt, kt, sl: (kt[t], vb + h)),
                  lvec, lvec, lvec, lvec,
                  pl.BlockSpec((1, HEAD_WIDTH), lambda h, t, qt, kt, sl: (0, 0))],
        out_specs=pl.BlockSpec((t_tile, HEAD_WIDTH), lambda h, t, qt, kt, sl: (qt[t], h)),
        scratch_shapes=[pltpu.VMEM((t_tile, HEAD_WIDTH), BF16),
                        pltpu.VMEM((2, t_tile, t_tile), F32),
                        pltpu.VMEM((2, t_tile, 128), F32),
                        pltpu.VMEM((2, t_tile, 128), F32),
                        pltpu.VMEM((2, t_tile, HEAD_WIDTH), F32)])
    r = lambda a: a.reshape(1, -1)
    return pl.pallas_call(
        _attn_prompt_kernel, grid_spec=grid_spec,
        out_shape=jax.ShapeDtypeStruct((m, B_WIDTH), BF16),
        compiler_params=_params("arbitrary", "arbitrary"), name="attn_prompt")(
            qi_tab, kj_tab, slopes, z, z, z, r(lq1), r(lk1), r(lq2), r(lk2), r(subln_g))


def _attn_sample_kernel(pt_ref, q_ref, kn_ref, vn_ref, slope_ref, lq1_ref, lk1_ref, lq2_ref, lk2_ref, sg_ref,
                        *rest):
    npg = PAGES_PER_STEP
    k_refs = rest[:npg]
    v_refs = rest[npg:2 * npg]
    o_ref, qmat_ref, m_ref, l_ref, acc_ref = rest[2 * npg:]
    s_id = pl.program_id(1)
    rows = 2 * N_HEADS

    @pl.when(s_id == 0)
    def _():
        q8 = q_ref[...] * Q_SCALE
        lane = lax.broadcasted_iota(jnp.int32, (N_HEADS, HEAD_WIDTH), 1)
        qmat = jnp.concatenate([jnp.where(lane < HEAD_DIM, q8, 0.0),
                                jnp.where(lane >= HEAD_DIM, q8, 0.0)], axis=0)
        qmat_ref[...] = qmat.astype(BF16)
        kn = kn_ref[...]
        vn = vn_ref[...]
        s_new = jnp.sum(qmat * jnp.concatenate([kn, kn], axis=0), axis=1, keepdims=True)
        m_ref[...] = jnp.broadcast_to(s_new, m_ref.shape)
        l_ref[...] = jnp.ones(l_ref.shape, F32)
        acc_ref[...] = jnp.concatenate([vn, vn], axis=0)

    qmat = qmat_ref[...]
    scores = []
    for i in range(npg):
        kp = k_refs[i][...].astype(BF16)
        scores.append(lax.dot_general(qmat, kp, (((1,), (1,)), ((), ())), preferred_element_type=F32))
    width = npg * PAGE_SIZE * N_HEADS
    col = lax.broadcasted_iota(jnp.int32, (rows, width), 1)
    row = lax.broadcasted_iota(jnp.int32, (rows, width), 0)
    pos = s_id * (npg * PAGE_SIZE) + col // N_HEADS
    bias = -slope_ref[:, :1] * (PAST_LEN - pos).astype(F32)
    s = jnp.where(col % N_HEADS == row % N_HEADS, jnp.concatenate(scores, axis=1) + bias, NEG_INF)
    m_prev = m_ref[...]
    m_next = jnp.maximum(m_prev, jnp.max(s, axis=1, keepdims=True))
    alpha = jnp.exp(m_prev - m_next)
    p = jnp.exp(s - m_next[:, :1])
    l_ref[...] = alpha * l_ref[...] + jnp.sum(p, axis=1, keepdims=True)
    m_ref[...] = m_next
    acc = acc_ref[...] * alpha[:, :1]
    pw = PAGE_SIZE * N_HEADS
    for i in range(npg):
        acc = acc + jnp.dot(p[:, i * pw:(i + 1) * pw].astype(BF16),
                            v_refs[i][...].astype(BF16), preferred_element_type=F32)
    acc_ref[...] = acc

    @pl.when(s_id == pl.num_programs(1) - 1)
    def _():
        lam = _lambda(lq1_ref, lk1_ref, lq2_ref, lk2_ref)
        o = acc_ref[...] / l_ref[...][:, :1]
        out = o[:N_HEADS] - lam * o[N_HEADS:]
        o_ref[...] = _rms(out, sg_ref[...]) * (1.0 - LAM_INIT)


def attn_sample(z, cache_k, cache_v, page_table, lq1, lk1, lq2, lk2, subln_g):
    nb = z.shape[0]
    npg = PAGES_PER_STEP
    n_steps = N_PAGES // npg
    n_pool = cache_k.shape[0]
    ck = cache_k.reshape(n_pool, PAGE_SIZE * N_HEADS, HEAD_WIDTH)
    cv = cache_v.reshape(n_pool, PAGE_SIZE * N_HEADS, HEAD_WIDTH)
    heads = lambda c0: z[:, c0:c0 + B_WIDTH].reshape(nb, N_HEADS, HEAD_WIDTH)
    slope16 = jnp.asarray(np.broadcast_to(np.tile(_alibi_slopes(), 2)[:, None], (2 * N_HEADS, 128)).copy())
    headblk = pl.BlockSpec((None, N_HEADS, HEAD_WIDTH), lambda b, s, pt: (b, 0, 0))
    lvec = pl.BlockSpec((1, HEAD_DIM), lambda b, s, pt: (0, 0))

    def page_spec(i):
        return pl.BlockSpec((None, PAGE_SIZE * N_HEADS, HEAD_WIDTH),
                            lambda b, s, pt: (pt[b * N_PAGES + s * npg + i], 0, 0))

    grid_spec = pltpu.PrefetchScalarGridSpec(
        num_scalar_prefetch=1, grid=(nb, n_steps),
        in_specs=[headblk, headblk, headblk,
                  pl.BlockSpec((2 * N_HEADS, 128), lambda b, s, pt: (0, 0)),
                  lvec, lvec, lvec, lvec,
                  pl.BlockSpec((1, HEAD_WIDTH), lambda b, s, pt: (0, 0))]
                 + [page_spec(i) for i in range(npg)] + [page_spec(i) for i in range(npg)],
        out_specs=headblk,
        scratch_shapes=[pltpu.VMEM((2 * N_HEADS, HEAD_WIDTH), BF16),
                        pltpu.VMEM((2 * N_HEADS, 128), F32),
                        pltpu.VMEM((2 * N_HEADS, 128), F32),
                        pltpu.VMEM((2 * N_HEADS, HEAD_WIDTH), F32)])
    r = lambda a: a.reshape(1, -1)
    out = pl.pallas_call(
        _attn_sample_kernel, grid_spec=grid_spec,
        out_shape=jax.ShapeDtypeStruct((nb, N_HEADS, HEAD_WIDTH), F32),
        compiler_params=_params("arbitrary", "arbitrary"), name="attn_sample")(
            page_table.reshape(-1), heads(COL_Q), heads(COL_K), heads(COL_V), slope16,
            r(lq1), r(lk1), r(lq2), r(lk2), r(subln_g), *([ck] * npg), *([cv] * npg))
    return out.reshape(nb, B_WIDTH)


def _merge_kernel(ya_ref, yb_ref, wa_ref, wb_ref, ga_ref, gb_ref, o_ref):
    a = jnp.dot(ya_ref[...], wa_ref[...].astype(BF16), preferred_element_type=F32)
    b = jnp.dot(yb_ref[...], wb_ref[...].astype(BF16), preferred_element_type=F32)
    o_ref[...] = (jax.nn.sigmoid(ga_ref[...]) * a + jax.nn.sigmoid(gb_ref[...]) * b).astype(o_ref.dtype)


def merge(ya, yb, z, w_a, w_b, tm, tn):
    m = ya.shape[0]
    act = lambda: pl.BlockSpec((tm, A_WIDTH), lambda i, j: (i, 0), pipeline_mode=pl.Buffered(1))
    wsp = pl.BlockSpec((A_WIDTH, tn), lambda i, j: (0, j))
    return pl.pallas_call(
        _merge_kernel, grid=(m // tm, D_MODEL // tn),
        in_specs=[act(), act(), wsp, wsp,
                  pl.BlockSpec((tm, tn), lambda i, j: (i, COL_GA // tn + j)),
                  pl.BlockSpec((tm, tn), lambda i, j: (i, COL_GB // tn + j))],
        out_specs=pl.BlockSpec((tm, tn), lambda i, j: (i, j)),
        out_shape=jax.ShapeDtypeStruct((m, D_MODEL), BF16),
        compiler_params=_params("arbitrary", "arbitrary"), name="merge")(ya, yb, w_a, w_b, z, z)


def _conv3(up, prev1, prev2, cw_ref, cb_ref):
    rows = lax.broadcasted_iota(jnp.int32, up.shape, 0)
    s1 = jnp.where(rows == 0, prev1, pltpu.roll(up, 1, axis=0))
    s2 = jnp.where(rows == 0, prev2, jnp.where(rows == 1, prev1, pltpu.roll(up, 2, axis=0)))
    return cb_ref[...] + cw_ref[0:1, :] * s2 + cw_ref[1:2, :] * s1 + cw_ref[2:3, :] * up


def _ffn_up_prompt_kernel(h_ref, wg_ref, wv_ref, cwg_ref, cwv_ref, cbg_ref, cbv_ref,
                          act_ref, tailg_ref, tailv_ref, carry_ref):
    i = pl.program_id(0)
    j = pl.program_id(1)
    h = h_ref[...]
    upg = jnp.dot(h, wg_ref[...].astype(BF16), preferred_element_type=F32)
    upv = jnp.dot(h, wv_ref[...].astype(BF16), preferred_element_type=F32)
    tm = upg.shape[0]

    @pl.when(i == 0)
    def _():
        carry_ref[j] = jnp.zeros(carry_ref.shape[1:], F32)

    prev = carry_ref[j]
    cg = _conv3(upg, prev[7:8, :FFN_TN], prev[6:7, :FFN_TN], cwg_ref, cbg_ref)
    cv = _conv3(upv, prev[7:8, FFN_TN:], prev[6:7, FFN_TN:], cwv_ref, cbv_ref)
    act_ref[...] = (_gelu(cg) * cv).astype(act_ref.dtype)
    tail_g = upg[tm - 8:, :]
    tail_v = upv[tm - 8:, :]
    carry_ref[j] = jnp.concatenate([tail_g, tail_v], axis=1)
    tailg_ref[...] = tail_g
    tailv_ref[...] = tail_v


def ffn_up_prompt(h, w_up, conv_w, conv_b, tm):
    m = h.shape[0]
    n_mt = m // tm
    np_ = FFN_PANELS
    cb = conv_b.reshape(1, 2 * D_FF)
    tail = pl.BlockSpec((None, 8, FFN_TN), lambda i, j: (i, 0, j))
    act, tail_g, tail_v = pl.pallas_call(
        _ffn_up_prompt_kernel, grid=(n_mt, np_),
        in_specs=[pl.BlockSpec((tm, D_MODEL), lambda i, j: (i, 0), pipeline_mode=pl.Buffered(1)),
                  pl.BlockSpec((D_MODEL, FFN_TN), lambda i, j: (0, j)),
                  pl.BlockSpec((D_MODEL, FFN_TN), lambda i, j: (0, np_ + j)),
                  pl.BlockSpec((CONV_W, FFN_TN), lambda i, j: (0, j)),
                  pl.BlockSpec((CONV_W, FFN_TN), lambda i, j: (0, np_ + j)),
                  pl.BlockSpec((1, FFN_TN), lambda i, j: (0, j)),
                  pl.BlockSpec((1, FFN_TN), lambda i, j: (0, np_ + j))],
        out_specs=[pl.BlockSpec((tm, FFN_TN), lambda i, j: (i, j)), tail, tail],
        out_shape=[jax.ShapeDtypeStruct((m, D_FF), BF16),
                   jax.ShapeDtypeStruct((n_mt, 8, D_FF), F32),
                   jax.ShapeDtypeStruct((n_mt, 8, D_FF), F32)],
        scratch_shapes=[pltpu.VMEM((np_, 8, 2 * FFN_TN), F32)],
        compiler_params=_params("arbitrary", "arbitrary"), name="ffn_up_prompt")(
            h, w_up, w_up, conv_w, conv_w, cb, cb)
    new_conv = jnp.concatenate([tail_g[n_mt - 1, 8 - (CONV_W - 1):], tail_v[n_mt - 1, 8 - (CONV_W - 1):]], axis=-1)
    return act, new_conv


def _ffn_up_sample_kernel(h_ref, wg_ref, wv_ref, cwg_ref, cwv_ref, cbg_ref, cbv_ref,
                          s0g_ref, s0v_ref, s1g_ref, s1v_ref, act_ref, upg_ref, upv_ref):
    h = h_ref[...]
    upg = jnp.dot(h, wg_ref[...].astype(BF16), preferred_element_type=F32)
    upv = jnp.dot(h, wv_ref[...].astype(BF16), preferred_element_type=F32)
    cg = cbg_ref[...] + cwg_ref[0:1, :] * s0g_ref[...] + cwg_ref[1:2, :] * s1g_ref[...] + cwg_ref[2:3, :] * upg
    cv = cbv_ref[...] + cwv_ref[0:1, :] * s0v_ref[...] + cwv_ref[1:2, :] * s1v_ref[...] + cwv_ref[2:3, :] * upv
    act_ref[...] = (_gelu(cg) * cv).astype(act_ref.dtype)
    upg_ref[...] = upg
    upv_ref[...] = upv


def ffn_up_sample(h, w_up, conv_w, conv_b, state):
    m = h.shape[0]
    np_ = FFN_PANELS
    cb = conv_b.reshape(1, 2 * D_FF)
    st = state.reshape(m, (CONV_W - 1) * 2 * D_FF)
    blk = lambda off: pl.BlockSpec((m, FFN_TN), lambda j: (0, off + j))
    act, upg, upv = pl.pallas_call(
        _ffn_up_sample_kernel, grid=(np_,),
        in_specs=[pl.BlockSpec((m, D_MODEL), lambda j: (0, 0)),
                  pl.BlockSpec((D_MODEL, FFN_TN), lambda j: (0, j)),
                  pl.BlockSpec((D_MODEL, FFN_TN), lambda j: (0, np_ + j)),
                  pl.BlockSpec((CONV_W, FFN_TN), lambda j: (0, j)),
                  pl.BlockSpec((CONV_W, FFN_TN), lambda j: (0, np_ + j)),
                  pl.BlockSpec((1, FFN_TN), lambda j: (0, j)),
                  pl.BlockSpec((1, FFN_TN), lambda j: (0, np_ + j)),
                  blk(0), blk(np_), blk(2 * np_), blk(3 * np_)],
        out_specs=[blk(0), blk(0), blk(0)],
        out_shape=[jax.ShapeDtypeStruct((m, D_FF), BF16),
                   jax.ShapeDtypeStruct((m, D_FF), F32),
                   jax.ShapeDtypeStruct((m, D_FF), F32)],
        compiler_params=_params("arbitrary"), name="ffn_up_sample")(
            h, w_up, w_up, conv_w, conv_w, cb, cb, st, st, st, st)
    up = jnp.concatenate([upg, upv], axis=-1)
    new_conv = jnp.concatenate([state[:, 1:], up[:, None, :]], axis=1)
    return act, new_conv


def _layer(x, p, attn_fn, branch_a_fn, ffn_up_fn, tm_big, tm_down, tm_row):
    h = rms_cast(x, p["g_pre_mix"], tm_row)
    z = matmul(h, p["w_in"], tm_big, 512, "in_proj")
    ya_out = branch_a_fn(z, p["a_ln_g"], p["a_ln_b"], p["a_w_s"], p["a_b_s"])
    ya = ya_out[0] if isinstance(ya_out, (list, tuple)) else ya_out
    yb = attn_fn(z)
    mixed = merge(ya, yb, z, p["w_branch_a"], p["w_branch_b"], tm_big, 256)
    t = matmul(mixed, p["w_out"], tm_big, 512, "out_proj")
    x1, h2 = post_mix(x, t, p["g_post_mix"], p["g_pre_ffn"], tm_row)
    act, new_conv = ffn_up_fn(h2)
    t2 = matmul(act, p["w_down"], tm_down, 256, "down_proj")
    y = post_ffn(x1, t2, p["g_post_ffn"], tm_row)
    return y, z, ya_out, new_conv


def kernel(x_prompt, x_sample, cache_k, cache_v, state_ffn_conv, page_table, g_pre_mix, w_in, a_ln_g, a_ln_b, a_w_s, a_b_s, lambda_q1, lambda_k1, lambda_q2, lambda_k2, subln_g, w_branch_a, w_branch_b, w_out, g_post_mix, g_pre_ffn, w_up, conv_w, conv_b, w_down, g_post_ffn):
    l = 0
    p = dict(g_pre_mix=g_pre_mix[l], w_in=w_in[l], a_ln_g=a_ln_g[l], a_ln_b=a_ln_b[l], a_w_s=a_w_s[l],
             a_b_s=a_b_s[l], w_branch_a=w_branch_a[l], w_branch_b=w_branch_b[l], w_out=w_out[l],
             g_post_mix=g_post_mix[l], g_pre_ffn=g_pre_ffn[l], w_up=w_up[l], conv_w=conv_w[l],
             conv_b=conv_b[l], w_down=w_down[l], g_post_ffn=g_post_ffn[l])
    lams = (lambda_q1[l], lambda_k1[l], lambda_q2[l], lambda_k2[l], subln_g[l])

    xp = x_prompt.reshape(SEQ, D_MODEL)
    yp, zp, _, conv_p = _layer(
        xp, p,
        attn_fn=lambda z: attn_prompt(z, *lams),
        branch_a_fn=branch_a_prompt,
        ffn_up_fn=lambda h2: ffn_up_prompt(h2, p["w_up"], p["conv_w"], p["conv_b"], 2048),
        tm_big=2048, tm_down=1024, tm_row=256)

    xs = x_sample.reshape(DEC_BATCH, D_MODEL)
    ys, zs, ya_s, conv_s = _layer(
        xs, p,
        attn_fn=lambda z: attn_sample(z, cache_k[l], cache_v[l], page_table, *lams).astype(BF16),
        branch_a_fn=branch_a_sample,
        ffn_up_fn=lambda h2: ffn_up_sample(h2, p["w_up"], p["conv_w"], p["conv_b"], state_ffn_conv[l]),
        tm_big=DEC_BATCH, tm_down=DEC_BATCH, tm_row=DEC_BATCH)

    hs = (N_HEADS, HEAD_WIDTH)
    return (yp.reshape(1, SEQ, D_MODEL),
            ys.reshape(DEC_BATCH, 1, D_MODEL),
            zp[:, COL_K:COL_V].reshape(1, 1, SEQ, *hs),
            zp[:, COL_V:COL_GA].reshape(1, 1, SEQ, *hs),
            zs[:, COL_K:COL_V].reshape(1, DEC_BATCH, 1, *hs),
            zs[:, COL_V:COL_GA].reshape(1, DEC_BATCH, 1, *hs),
            ya_s[1].reshape(1, DEC_BATCH, 1, A_WIDTH),
            conv_p.reshape(1, 1, CONV_W - 1, 2 * D_FF),
            conv_s.reshape(1, DEC_BATCH, CONV_W - 1, 2 * D_FF))
```

```python
import functools
import math

import numpy as np
import jax
import jax.numpy as jnp
from jax import lax
from jax.experimental import pallas as pl
from jax.experimental.pallas import tpu as pltpu

D_MODEL = 4096
SEQ = 8192
DEC_BATCH = 128
PAST_LEN = 2048
PAGE_SIZE = 128
N_PAGES = PAST_LEN // PAGE_SIZE
A_WIDTH = D_MODEL // 2
A_GROUP_WIDTH = 128
A_GROUPS = A_WIDTH // A_GROUP_WIDTH
CHUNK = 128
HEAD_DIM = 128
HEAD_WIDTH = 2 * HEAD_DIM
N_HEADS = D_MODEL // (4 * HEAD_DIM)
B_WIDTH = N_HEADS * HEAD_WIDTH
D_FF = ((8 * D_MODEL // 3 + 255) // 256) * 256
CONV_W = 3
N_IN = 2 * A_WIDTH + 3 * B_WIDTH + 2 * D_MODEL
EPS = 1e-6
NEG_INF = -1e30
LAM_INIT = 0.8 - 0.6 * math.exp(-0.3 * 0)
Q_SCALE = HEAD_DIM ** -0.5
LOG2E = math.log2(math.e)

COL_U = 0
COL_VA = A_WIDTH
COL_Q = 2 * A_WIDTH
COL_K = COL_Q + B_WIDTH
COL_V = COL_K + B_WIDTH
COL_GA = COL_V + B_WIDTH
COL_GB = COL_GA + D_MODEL

V7X_VMEM_LIMIT_BYTES = 58 * 1024 * 1024
FFN_TN = 256
FFN_PANELS = D_FF // FFN_TN
FFN_SUB_ROWS = 512
ATT_T = 512
PAGES_PER_STEP = 8

BF16 = jnp.bfloat16
F32 = jnp.float32


def _params(*sem):
    return pltpu.CompilerParams(dimension_semantics=sem, vmem_limit_bytes=V7X_VMEM_LIMIT_BYTES)


def _gelu(x):
    return 0.5 * x * (1.0 + jnp.tanh(0.7978845608028654 * (x + 0.044715 * (x * x * x))))


def _rms(x, g):
    return x * lax.rsqrt(jnp.mean(x * x, axis=-1, keepdims=True) + EPS) * g


def _rms_cast_kernel(x_ref, g_ref, o_ref):
    o_ref[...] = _rms(x_ref[...], g_ref[...]).astype(o_ref.dtype)


def rms_cast(x, g, tm):
    m, d = x.shape
    return pl.pallas_call(
        _rms_cast_kernel, grid=(m // tm,),
        in_specs=[pl.BlockSpec((tm, d), lambda i: (i, 0)), pl.BlockSpec((1, d), lambda i: (0, 0))],
        out_specs=pl.BlockSpec((tm, d), lambda i: (i, 0)),
        out_shape=jax.ShapeDtypeStruct((m, d), BF16),
        compiler_params=_params("arbitrary"), name="rms_cast")(x, g.reshape(1, d))


def _post_mix_kernel(x_ref, t_ref, g1_ref, g2_ref, x1_ref, h_ref):
    x1 = x_ref[...] + _rms(t_ref[...], g1_ref[...])
    x1_ref[...] = x1
    h_ref[...] = _rms(x1, g2_ref[...]).astype(h_ref.dtype)


def post_mix(x, t, g_post, g_pre_next, tm):
    m, d = x.shape
    row = pl.BlockSpec((tm, d), lambda i: (i, 0))
    vec = pl.BlockSpec((1, d), lambda i: (0, 0))
    return pl.pallas_call(
        _post_mix_kernel, grid=(m // tm,), in_specs=[row, row, vec, vec], out_specs=[row, row],
        out_shape=[jax.ShapeDtypeStruct((m, d), F32), jax.ShapeDtypeStruct((m, d), BF16)],
        compiler_params=_params("arbitrary"), name="post_mix")(
            x, t, g_post.reshape(1, d), g_pre_next.reshape(1, d))


def _post_ffn_kernel(x_ref, t_ref, g_ref, y_ref):
    y_ref[...] = x_ref[...] + _rms(t_ref[...], g_ref[...])


def post_ffn(x, t, g, tm):
    m, d = x.shape
    row = pl.BlockSpec((tm, d), lambda i: (i, 0))
    vec = pl.BlockSpec((1, d), lambda i: (0, 0))
    return pl.pallas_call(
        _post_ffn_kernel, grid=(m // tm,), in_specs=[row, row, vec], out_specs=row,
        out_shape=jax.ShapeDtypeStruct((m, d), F32),
        compiler_params=_params("arbitrary"), name="post_ffn")(x, t, g.reshape(1, d))


def _mm_kernel(a_ref, w_ref, o_ref):
    o_ref[...] = jnp.dot(a_ref[...], w_ref[...].astype(BF16),
                         preferred_element_type=F32).astype(o_ref.dtype)


def matmul(a, w, tm, tn, name):
    m, k = a.shape
    n = w.shape[1]
    return pl.pallas_call(
        _mm_kernel, grid=(m // tm, n // tn),
        in_specs=[pl.BlockSpec((tm, k), lambda i, j: (i, 0), pipeline_mode=pl.Buffered(1)),
                  pl.BlockSpec((k, tn), lambda i, j: (0, j))],
        out_specs=pl.BlockSpec((tm, tn), lambda i, j: (i, j)),
        out_shape=jax.ShapeDtypeStruct((m, n), F32),
        compiler_params=_params("arbitrary", "arbitrary"), name=name)(a, w)


def _layernorm(x, g, b):
    mu = jnp.mean(x, axis=-1, keepdims=True)
    xc = x - mu
    var = jnp.mean(xc * xc, axis=-1, keepdims=True)
    return xc * lax.rsqrt(var + EPS) * g + b


def _branch_a_prompt_kernel(u_ref, va_ref, lng_ref, lnb_ref, ws_ref, bst_ref, ya_ref):
    u = _gelu(u_ref[...])
    vn = _layernorm(_gelu(va_ref[...]), lng_ref[...], lnb_ref[...])
    row = lax.broadcasted_iota(jnp.int32, (CHUNK, CHUNK), 0)
    col = lax.broadcasted_iota(jnp.int32, (CHUNK, CHUNK), 1)
    causal = row >= col
    for g in range(A_GROUPS):
        sl = slice(g * A_GROUP_WIDTH, (g + 1) * A_GROUP_WIDTH)
        w = jnp.where(causal, ws_ref[g], 0.0).astype(BF16)
        mixed = jnp.dot(w, vn[:, sl].astype(BF16), preferred_element_type=F32) + bst_ref[:, g:g + 1]
        ya_ref[:, sl] = (u[:, sl] * mixed).astype(ya_ref.dtype)


def branch_a_prompt(z, ln_g, ln_b, w_s, b_s):
    m = z.shape[0]
    vec = pl.BlockSpec((1, A_WIDTH), lambda i: (0, 0))
    return pl.pallas_call(
        _branch_a_prompt_kernel, grid=(m // CHUNK,),
        in_specs=[pl.BlockSpec((CHUNK, A_WIDTH), lambda i: (i, COL_U // A_WIDTH)),
                  pl.BlockSpec((CHUNK, A_WIDTH), lambda i: (i, COL_VA // A_WIDTH)),
                  vec, vec,
                  pl.BlockSpec((A_GROUPS, CHUNK, CHUNK), lambda i: (0, 0, 0)),
                  pl.BlockSpec((CHUNK, A_GROUPS), lambda i: (0, 0))],
        out_specs=pl.BlockSpec((CHUNK, A_WIDTH), lambda i: (i, 0)),
        out_shape=jax.ShapeDtypeStruct((m, A_WIDTH), BF16),
        compiler_params=_params("arbitrary"), name="branch_a_prompt")(
            z, z, ln_g.reshape(1, A_WIDTH), ln_b.reshape(1, A_WIDTH), w_s, b_s.T)


def _branch_a_sample_kernel(u_ref, va_ref, lng_ref, lnb_ref, w00_ref, b0_ref, ya_ref, vn_ref):
    u = _gelu(u_ref[...])
    vn = _layernorm(_gelu(va_ref[...]), lng_ref[...], lnb_ref[...])
    vn_ref[...] = vn
    ya_ref[...] = (u * (vn * w00_ref[...] + b0_ref[...])).astype(ya_ref.dtype)


def branch_a_sample(z, ln_g, ln_b, w_s, b_s):
    m = z.shape[0]
    w00 = jnp.repeat(w_s[:, 0, 0], A_GROUP_WIDTH).reshape(1, A_WIDTH)
    b0 = jnp.repeat(b_s[:, 0], A_GROUP_WIDTH).reshape(1, A_WIDTH)
    vec = pl.BlockSpec((1, A_WIDTH), lambda i: (0, 0))
    out = pl.BlockSpec((m, A_WIDTH), lambda i: (0, 0))
    return pl.pallas_call(
        _branch_a_sample_kernel, grid=(1,),
        in_specs=[pl.BlockSpec((m, A_WIDTH), lambda i: (0, COL_U // A_WIDTH)),
                  pl.BlockSpec((m, A_WIDTH), lambda i: (0, COL_VA // A_WIDTH)),
                  vec, vec, vec, vec],
        out_specs=[out, out],
        out_shape=[jax.ShapeDtypeStruct((m, A_WIDTH), BF16), jax.ShapeDtypeStruct((m, A_WIDTH), F32)],
        compiler_params=_params("arbitrary"), name="branch_a_sample")(
            z, z, ln_g.reshape(1, A_WIDTH), ln_b.reshape(1, A_WIDTH), w00, b0)


def _lambda(lq1_ref, lk1_ref, lq2_ref, lk2_ref):
    return (jnp.exp(jnp.sum(lq1_ref[...] * lk1_ref[...], axis=1, keepdims=True))
            - jnp.exp(jnp.sum(lq2_ref[...] * lk2_ref[...], axis=1, keepdims=True)) + LAM_INIT)


def _alibi_slopes():
    return np.asarray(2.0 ** (-8.0 * np.arange(1, N_HEADS + 1, dtype=np.float32) / N_HEADS), dtype=np.float32)


N_C_PIECES = 3
EXTRA_COLS = 4 * N_C_PIECES


def _alibi_pieces():
    out = []
    for s in _alibi_slopes().astype(np.float64):
        rest = s * math.log2(math.e)
        for _ in range(N_C_PIECES):
            piece = float(np.asarray(rest, np.float32).astype(BF16).astype(np.float32))
            out.append(piece)
            rest -= piece
    return np.asarray(out, np.float32)


def _split_pos(pos):
    lo = pos % 256
    return lo.astype(F32), (pos - lo).astype(F32)


def _qkv_prep_kernel(c_ref, q_ref, k_ref, v_ref, qt_ref, ka_ref, vt_ref):
    t_tile = ATT_T
    h = pl.program_id(1)
    cs = [c_ref[h * N_C_PIECES + t] for t in range(N_C_PIECES)]

    qt = (q_ref[...] * (Q_SCALE * LOG2E)).T
    row = lax.broadcasted_iota(jnp.int32, (HEAD_DIM, t_tile), 0)
    i_lo, i_hi = _split_pos(lax.broadcasted_iota(jnp.int32, (HEAD_DIM, t_tile), 1))
    qe = jnp.zeros((HEAD_DIM, t_tile), F32)
    for t in range(N_C_PIECES):
        qe = jnp.where(row == 4 * t, i_lo, qe)
        qe = jnp.where(row == 4 * t + 1, i_hi, qe)
        qe = jnp.where((row == 4 * t + 2) | (row == 4 * t + 3), cs[t], qe)
    qt_ref[0] = jnp.concatenate([qt[:HEAD_DIM], qe], axis=0).astype(BF16)
    qt_ref[1] = jnp.concatenate([qt[HEAD_DIM:], qe], axis=0).astype(BF16)

    k = k_ref[...]
    col = lax.broadcasted_iota(jnp.int32, (t_tile, HEAD_DIM), 1)
    j_lo, j_hi = _split_pos(lax.broadcasted_iota(jnp.int32, (t_tile, HEAD_DIM), 0))
    ke = jnp.zeros((t_tile, HEAD_DIM), F32)
    for t in range(N_C_PIECES):
        ke = jnp.where((col == 4 * t) | (col == 4 * t + 1), -cs[t], ke)
        ke = jnp.where(col == 4 * t + 2, j_lo, ke)
        ke = jnp.where(col == 4 * t + 3, j_hi, ke)
    ka_ref[0] = jnp.concatenate([k[:, :HEAD_DIM], ke], axis=1).astype(BF16)
    ka_ref[1] = jnp.concatenate([k[:, HEAD_DIM:], ke], axis=1).astype(BF16)

    vt_ref[...] = v_ref[...].T.astype(BF16)


def qkv_prep(z, seq):
    t_tile = ATT_T
    nt = seq // t_tile
    qb, kb, vb = COL_Q // HEAD_WIDTH, COL_K // HEAD_WIDTH, COL_V // HEAD_WIDTH
    grid_spec = pltpu.PrefetchScalarGridSpec(
        num_scalar_prefetch=1, grid=(nt, N_HEADS),
        in_specs=[pl.BlockSpec((t_tile, HEAD_WIDTH), lambda i, h, c: (i, qb + h)),
                  pl.BlockSpec((t_tile, HEAD_WIDTH), lambda i, h, c: (i, kb + h)),
                  pl.BlockSpec((t_tile, HEAD_WIDTH), lambda i, h, c: (i, vb + h))],
        out_specs=[pl.BlockSpec((None, 2, HEAD_WIDTH, t_tile), lambda i, h, c: (h, 0, 0, i)),
                   pl.BlockSpec((None, 2, None, t_tile, HEAD_WIDTH), lambda i, h, c: (h, 0, i, 0, 0)),
                   pl.BlockSpec((None, None, HEAD_WIDTH, t_tile), lambda i, h, c: (h, i, 0, 0))])
    return pl.pallas_call(
        _qkv_prep_kernel, grid_spec=grid_spec,
        out_shape=[jax.ShapeDtypeStruct((N_HEADS, 2, HEAD_WIDTH, seq), BF16),
                   jax.ShapeDtypeStruct((N_HEADS, 2, nt, t_tile, HEAD_WIDTH), BF16),
                   jax.ShapeDtypeStruct((N_HEADS, nt, HEAD_WIDTH, t_tile), BF16)],
        compiler_params=_params("arbitrary", "arbitrary"), name="qkv_prep")(
            jnp.asarray(_alibi_pieces()), z, z, z)


HEAD_GROUPS = 2


def _attn_apply_pending(kv, vt_refs, p_ref, al_ref, acc_ref):
    for g in range(HEAD_GROUPS):
        vt = vt_refs[g][kv]
        for half in range(2):
            n = 2 * g + half
            acc_ref[n] = acc_ref[n] * al_ref[n] + jnp.dot(vt, p_ref[n], preferred_element_type=F32)


def _attn_scores(kj, coffs, masked, qt_refs, ka_refs, p_ref, al_ref, stats):
    t_tile = ATT_T
    new_stats = []
    for g in range(HEAD_GROUPS):
        for half in range(2):
            n = 2 * g + half
            s = jnp.dot(ka_refs[g][half, kj], qt_refs[g][half], preferred_element_type=F32)
            if masked:
                key = lax.broadcasted_iota(jnp.int32, (t_tile, t_tile), 0)
                qry = lax.broadcasted_iota(jnp.int32, (t_tile, t_tile), 1)
                s = jnp.where(key > qry, NEG_INF, s)
            m_prev, l_prev = stats[2 * n], stats[2 * n + 1]
            m_next = jnp.maximum(m_prev, jnp.max(s, axis=0, keepdims=True) + coffs[g])
            alpha = jnp.exp2(m_prev - m_next)
            p = jnp.exp2(s - (m_next - coffs[g]))
            new_stats += [m_next, alpha * l_prev + jnp.sum(p, axis=0, keepdims=True)]
            p_ref[n] = p.astype(BF16)
            al_ref[n] = alpha
    return tuple(new_stats)


def _attn_prompt_kernel(c_ref, qt_a, qt_b, ka_a, ka_b, vt_a, vt_b, lq1_ref, lk1_ref, lq2_ref, lk2_ref, sg_ref,
                        o_a, o_b, acc_ref, p_ref, al_ref):
    t_tile = ATT_T
    qt_refs, ka_refs, vt_refs, o_refs = (qt_a, qt_b), (ka_a, ka_b), (vt_a, vt_b), (o_a, o_b)
    hp = pl.program_id(0)
    qi = pl.program_id(1)
    c_sums = []
    for g in range(HEAD_GROUPS):
        base = (hp + g * (N_HEADS // HEAD_GROUPS)) * N_C_PIECES
        c_sum = c_ref[base]
        for t in range(1, N_C_PIECES):
            c_sum = c_sum + c_ref[base + t]
        c_sums.append(c_sum)

    acc_ref[...] = jnp.zeros(acc_ref.shape, F32)
    p_ref[...] = jnp.zeros(p_ref.shape, BF16)
    al_ref[...] = jnp.ones(al_ref.shape, F32)
    neg = jnp.full((1, t_tile), NEG_INF, F32)
    zero = jnp.zeros((1, t_tile), F32)

    def body(kj, stats):
        dist = ((qi - kj) * t_tile).astype(F32)
        _attn_apply_pending(jnp.maximum(kj - 1, 0), vt_refs, p_ref, al_ref, acc_ref)
        return _attn_scores(kj, [-c * dist for c in c_sums], False, qt_refs, ka_refs, p_ref, al_ref, stats)

    stats = lax.fori_loop(0, qi, body, (neg, zero) * (2 * HEAD_GROUPS))
    _attn_apply_pending(jnp.maximum(qi - 1, 0), vt_refs, p_ref, al_ref, acc_ref)
    stats = _attn_scores(qi, [0.0] * HEAD_GROUPS, True, qt_refs, ka_refs, p_ref, al_ref, stats)
    _attn_apply_pending(qi, vt_refs, p_ref, al_ref, acc_ref)

    lam = _lambda(lq1_ref, lk1_ref, lq2_ref, lk2_ref)
    for g in range(HEAD_GROUPS):
        l1, l2 = stats[4 * g + 1], stats[4 * g + 3]
        out = acc_ref[2 * g] / l1 - lam * (acc_ref[2 * g + 1] / l2)
        ms = jnp.mean(out * out, axis=0, keepdims=True)
        y = out * lax.rsqrt(ms + EPS) * sg_ref[...] * (1.0 - LAM_INIT)
        o_refs[g][...] = y.T.astype(o_refs[g].dtype)


def attn_prompt(z, seq, lq1, lk1, lq2, lk2, subln_g):
    t_tile = ATT_T
    nt = seq // t_tile
    hg = N_HEADS // HEAD_GROUPS
    qt, ka, vt = qkv_prep(z, seq)
    lvec = pl.BlockSpec((1, HEAD_DIM), lambda h, i, c: (0, 0))
    qspec = lambda g: pl.BlockSpec((None, 2, HEAD_WIDTH, t_tile), lambda h, i, c: (h + g * hg, 0, 0, i))
    kspec = lambda g: pl.BlockSpec((None, 2, nt, t_tile, HEAD_WIDTH), lambda h, i, c: (h + g * hg, 0, 0, 0, 0),
                                   pipeline_mode=pl.Buffered(1))
    vspec = lambda g: pl.BlockSpec((None, nt, HEAD_WIDTH, t_tile), lambda h, i, c: (h + g * hg, 0, 0, 0),
                                   pipeline_mode=pl.Buffered(1))
    ospec = pl.BlockSpec((t_tile, HEAD_WIDTH), lambda h, i, c: (i, h))
    n_chain = 2 * HEAD_GROUPS
    grid_spec = pltpu.PrefetchScalarGridSpec(
        num_scalar_prefetch=1, grid=(hg, nt),
        in_specs=[qspec(0), qspec(1), kspec(0), kspec(1), vspec(0), vspec(1),
                  lvec, lvec, lvec, lvec,
                  pl.BlockSpec((HEAD_WIDTH, 1), lambda h, i, c: (0, 0))],
        out_specs=[ospec, ospec],
        scratch_shapes=[pltpu.VMEM((n_chain, HEAD_WIDTH, t_tile), F32),
                        pltpu.VMEM((n_chain, t_tile, t_tile), BF16),
                        pltpu.VMEM((n_chain, 1, t_tile), F32)])
    r = lambda a: a.reshape(1, -1)
    half_out = jax.ShapeDtypeStruct((seq, B_WIDTH // HEAD_GROUPS), BF16)
    y_a, y_b = pl.pallas_call(
        _attn_prompt_kernel, grid_spec=grid_spec, out_shape=[half_out, half_out],
        compiler_params=_params("arbitrary", "arbitrary"), name="attn_prompt")(
            jnp.asarray(_alibi_pieces()), qt, qt, ka, ka, vt, vt, r(lq1), r(lk1), r(lq2), r(lk2),
            subln_g.reshape(HEAD_WIDTH, 1))
    return jnp.concatenate([y_a, y_b], axis=1)


def _attn_sample_kernel(pt_ref, q_ref, kn_ref, vn_ref, slope_ref, lq1_ref, lk1_ref, lq2_ref, lk2_ref, sg_ref,
                        *rest):
    npg = PAGES_PER_STEP
    k_refs = rest[:npg]
    v_refs = rest[npg:2 * npg]
    o_ref, qmat_ref, m_ref, l_ref, acc_ref = rest[2 * npg:]
    s_id = pl.program_id(1)
    rows = 2 * N_HEADS

    @pl.when(s_id == 0)
    def _():
        q8 = q_ref[...] * Q_SCALE
        lane = lax.broadcasted_iota(jnp.int32, (N_HEADS, HEAD_WIDTH), 1)
        qmat = jnp.concatenate([jnp.where(lane < HEAD_DIM, q8, 0.0),
                                jnp.where(lane >= HEAD_DIM, q8, 0.0)], axis=0)
        qmat_ref[...] = qmat.astype(BF16)
        kn = kn_ref[...]
        vn = vn_ref[...]
        s_new = jnp.sum(qmat * jnp.concatenate([kn, kn], axis=0), axis=1, keepdims=True)
        m_ref[...] = jnp.broadcast_to(s_new, m_ref.shape)
        l_ref[...] = jnp.ones(l_ref.shape, F32)
        acc_ref[...] = jnp.concatenate([vn, vn], axis=0)

    qmat = qmat_ref[...]
    scores = []
    for i in range(npg):
        kp = k_refs[i][...].astype(BF16)
        scores.append(lax.dot_general(qmat, kp, (((1,), (1,)), ((), ())), preferred_element_type=F32))
    width = npg * PAGE_SIZE * N_HEADS
    col = lax.broadcasted_iota(jnp.int32, (rows, width), 1)
    row = lax.broadcasted_iota(jnp.int32, (rows, width), 0)
    pos = s_id * (npg * PAGE_SIZE) + col // N_HEADS
    bias = -slope_ref[:, :1] * (PAST_LEN - pos).astype(F32)
    s = jnp.where(col % N_HEADS == row % N_HEADS, jnp.concatenate(scores, axis=1) + bias, NEG_INF)
    m_prev = m_ref[...]
    m_next = jnp.maximum(m_prev, jnp.max(s, axis=1, keepdims=True))
    alpha = jnp.exp(m_prev - m_next)
    p = jnp.exp(s - m_next[:, :1])
    l_ref[...] = alpha * l_ref[...] + jnp.sum(p, axis=1, keepdims=True)
    m_ref[...] = m_next
    acc = acc_ref[...] * alpha[:, :1]
    pw = PAGE_SIZE * N_HEADS
    for i in range(npg):
        acc = acc + jnp.dot(p[:, i * pw:(i + 1) * pw].astype(BF16),
                            v_refs[i][...].astype(BF16), preferred_element_type=F32)
    acc_ref[...] = acc

    @pl.when(s_id == pl.num_programs(1) - 1)
    def _():
        lam = _lambda(lq1_ref, lk1_ref, lq2_ref, lk2_ref)
        o = acc_ref[...] / l_ref[...][:, :1]
        out = o[:N_HEADS] - lam * o[N_HEADS:]
        o_ref[...] = _rms(out, sg_ref[...]) * (1.0 - LAM_INIT)


def attn_sample(z, cache_k, cache_v, page_table, lq1, lk1, lq2, lk2, subln_g):
    nb = z.shape[0]
    npg = PAGES_PER_STEP
    n_steps = N_PAGES // npg
    n_pool = cache_k.shape[0]
    ck = cache_k.reshape(n_pool, PAGE_SIZE * N_HEADS, HEAD_WIDTH)
    cv = cache_v.reshape(n_pool, PAGE_SIZE * N_HEADS, HEAD_WIDTH)
    heads = lambda c0: z[:, c0:c0 + B_WIDTH].reshape(nb, N_HEADS, HEAD_WIDTH)
    slope16 = jnp.asarray(np.broadcast_to(np.tile(_alibi_slopes(), 2)[:, None], (2 * N_HEADS, 128)).copy())
    headblk = pl.BlockSpec((None, N_HEADS, HEAD_WIDTH), lambda b, s, pt: (b, 0, 0))
    lvec = pl.BlockSpec((1, HEAD_DIM), lambda b, s, pt: (0, 0))

    def page_spec(i):
        return pl.BlockSpec((None, PAGE_SIZE * N_HEADS, HEAD_WIDTH),
                            lambda b, s, pt: (pt[b * N_PAGES + s * npg + i], 0, 0))

    grid_spec = pltpu.PrefetchScalarGridSpec(
        num_scalar_prefetch=1, grid=(nb, n_steps),
        in_specs=[headblk, headblk, headblk,
                  pl.BlockSpec((2 * N_HEADS, 128), lambda b, s, pt: (0, 0)),
                  lvec, lvec, lvec, lvec,
                  pl.BlockSpec((1, HEAD_WIDTH), lambda b, s, pt: (0, 0))]
                 + [page_spec(i) for i in range(npg)] + [page_spec(i) for i in range(npg)],
        out_specs=headblk,
        scratch_shapes=[pltpu.VMEM((2 * N_HEADS, HEAD_WIDTH), BF16),
                        pltpu.VMEM((2 * N_HEADS, 128), F32),
                        pltpu.VMEM((2 * N_HEADS, 128), F32),
                        pltpu.VMEM((2 * N_HEADS, HEAD_WIDTH), F32)])
    r = lambda a: a.reshape(1, -1)
    out = pl.pallas_call(
        _attn_sample_kernel, grid_spec=grid_spec,
        out_shape=jax.ShapeDtypeStruct((nb, N_HEADS, HEAD_WIDTH), F32),
        compiler_params=_params("arbitrary", "arbitrary"), name="attn_sample")(
            page_table.reshape(-1), heads(COL_Q), heads(COL_K), heads(COL_V), slope16,
            r(lq1), r(lk1), r(lq2), r(lk2), r(subln_g), *([ck] * npg), *([cv] * npg))
    return out.reshape(nb, B_WIDTH)


def _merge_kernel(ya_ref, yb_ref, wa_ref, wb_ref, ga_ref, gb_ref, o_ref):
    a = jnp.dot(ya_ref[...], wa_ref[...].astype(BF16), preferred_element_type=F32)
    b = jnp.dot(yb_ref[...], wb_ref[...].astype(BF16), preferred_element_type=F32)
    o_ref[...] = (jax.nn.sigmoid(ga_ref[...]) * a + jax.nn.sigmoid(gb_ref[...]) * b).astype(o_ref.dtype)


def merge(ya, yb, z, w_a, w_b, tm, tn):
    m = ya.shape[0]
    act = lambda: pl.BlockSpec((tm, A_WIDTH), lambda i, j: (i, 0), pipeline_mode=pl.Buffered(1))
    wsp = pl.BlockSpec((A_WIDTH, tn), lambda i, j: (0, j))
    return pl.pallas_call(
        _merge_kernel, grid=(m // tm, D_MODEL // tn),
        in_specs=[act(), act(), wsp, wsp,
                  pl.BlockSpec((tm, tn), lambda i, j: (i, COL_GA // tn + j)),
                  pl.BlockSpec((tm, tn), lambda i, j: (i, COL_GB // tn + j))],
        out_specs=pl.BlockSpec((tm, tn), lambda i, j: (i, j)),
        out_shape=jax.ShapeDtypeStruct((m, D_MODEL), BF16),
        compiler_params=_params("arbitrary", "arbitrary"), name="merge")(ya, yb, w_a, w_b, z, z)


def _conv3(up, prev1, prev2, cw_ref, cb_ref):
    rows = lax.broadcasted_iota(jnp.int32, up.shape, 0)
    s1 = jnp.where(rows == 0, prev1, pltpu.roll(up, 1, axis=0))
    s2 = jnp.where(rows == 0, prev2, jnp.where(rows == 1, prev1, pltpu.roll(up, 2, axis=0)))
    return cb_ref[...] + cw_ref[0:1, :] * s2 + cw_ref[1:2, :] * s1 + cw_ref[2:3, :] * up


def _ffn_up_prompt_kernel(h_ref, wg_ref, wv_ref, cwg_ref, cwv_ref, cbg_ref, cbv_ref,
                          act_ref, tailg_ref, tailv_ref, carry_ref):
    i = pl.program_id(0)
    j = pl.program_id(1)
    wg = wg_ref[...].astype(BF16)
    wv = wv_ref[...].astype(BF16)

    @pl.when(i == 0)
    def _():
        carry_ref[j] = jnp.zeros(carry_ref.shape[1:], F32)

    prev = carry_ref[j]
    pg1, pg2 = prev[7:8, :FFN_TN], prev[6:7, :FFN_TN]
    pv1, pv2 = prev[7:8, FFN_TN:], prev[6:7, FFN_TN:]
    sub = FFN_SUB_ROWS
    for r in range(h_ref.shape[0] // sub):
        rows = slice(r * sub, (r + 1) * sub)
        h = h_ref[rows, :]
        upg = jnp.dot(h, wg, preferred_element_type=F32)
        upv = jnp.dot(h, wv, preferred_element_type=F32)
        cg = _conv3(upg, pg1, pg2, cwg_ref, cbg_ref)
        cv = _conv3(upv, pv1, pv2, cwv_ref, cbv_ref)
        act_ref[rows, :] = (_gelu(cg) * cv).astype(act_ref.dtype)
        pg1, pg2 = upg[sub - 1:sub, :], upg[sub - 2:sub - 1, :]
        pv1, pv2 = upv[sub - 1:sub, :], upv[sub - 2:sub - 1, :]
    tail_g = upg[sub - 8:, :]
    tail_v = upv[sub - 8:, :]
    carry_ref[j] = jnp.concatenate([tail_g, tail_v], axis=1)
    tailg_ref[...] = tail_g
    tailv_ref[...] = tail_v


def ffn_up_prompt(h, w_up, conv_w, conv_b, tm):
    m = h.shape[0]
    n_mt = m // tm
    np_ = FFN_PANELS
    cb = conv_b.reshape(1, 2 * D_FF)
    tail = pl.BlockSpec((None, 8, FFN_TN), lambda i, j: (i, 0, j))
    act, tail_g, tail_v = pl.pallas_call(
        _ffn_up_prompt_kernel, grid=(n_mt, np_),
        in_specs=[pl.BlockSpec((tm, D_MODEL), lambda i, j: (i, 0), pipeline_mode=pl.Buffered(1)),
                  pl.BlockSpec((D_MODEL, FFN_TN), lambda i, j: (0, j)),
                  pl.BlockSpec((D_MODEL, FFN_TN), lambda i, j: (0, np_ + j)),
                  pl.BlockSpec((CONV_W, FFN_TN), lambda i, j: (0, j)),
                  pl.BlockSpec((CONV_W, FFN_TN), lambda i, j: (0, np_ + j)),
                  pl.BlockSpec((1, FFN_TN), lambda i, j: (0, j)),
                  pl.BlockSpec((1, FFN_TN), lambda i, j: (0, np_ + j))],
        out_specs=[pl.BlockSpec((tm, FFN_TN), lambda i, j: (i, j)), tail, tail],
        out_shape=[jax.ShapeDtypeStruct((m, D_FF), BF16),
                   jax.ShapeDtypeStruct((n_mt, 8, D_FF), F32),
                   jax.ShapeDtypeStruct((n_mt, 8, D_FF), F32)],
        scratch_shapes=[pltpu.VMEM((np_, 8, 2 * FFN_TN), F32)],
        compiler_params=_params("arbitrary", "arbitrary"), name="ffn_up_prompt")(
            h, w_up, w_up, conv_w, conv_w, cb, cb)
    new_conv = jnp.concatenate([tail_g[n_mt - 1, 8 - (CONV_W - 1):], tail_v[n_mt - 1, 8 - (CONV_W - 1):]], axis=-1)
    return act, new_conv


def _ffn_up_sample_kernel(h_ref, wg_ref, wv_ref, cwg_ref, cwv_ref, cbg_ref, cbv_ref,
                          s0g_ref, s0v_ref, s1g_ref, s1v_ref, act_ref, upg_ref, upv_ref):
    h = h_ref[...]
    upg = jnp.dot(h, wg_ref[...].astype(BF16), preferred_element_type=F32)
    upv = jnp.dot(h, wv_ref[...].astype(BF16), preferred_element_type=F32)
    cg = cbg_ref[...] + cwg_ref[0:1, :] * s0g_ref[...] + cwg_ref[1:2, :] * s1g_ref[...] + cwg_ref[2:3, :] * upg
    cv = cbv_ref[...] + cwv_ref[0:1, :] * s0v_ref[...] + cwv_ref[1:2, :] * s1v_ref[...] + cwv_ref[2:3, :] * upv
    act_ref[...] = (_gelu(cg) * cv).astype(act_ref.dtype)
    upg_ref[...] = upg
    upv_ref[...] = upv


def ffn_up_sample(h, w_up, conv_w, conv_b, state):
    m = h.shape[0]
    np_ = FFN_PANELS
    cb = conv_b.reshape(1, 2 * D_FF)
    st = state.reshape(m, (CONV_W - 1) * 2 * D_FF)
    blk = lambda off: pl.BlockSpec((m, FFN_TN), lambda j: (0, off + j))
    act, upg, upv = pl.pallas_call(
        _ffn_up_sample_kernel, grid=(np_,),
        in_specs=[pl.BlockSpec((m, D_MODEL), lambda j: (0, 0)),
                  pl.BlockSpec((D_MODEL, FFN_TN), lambda j: (0, j)),
                  pl.BlockSpec((D_MODEL, FFN_TN), lambda j: (0, np_ + j)),
                  pl.BlockSpec((CONV_W, FFN_TN), lambda j: (0, j)),
                  pl.BlockSpec((CONV_W, FFN_TN), lambda j: (0, np_ + j)),
                  pl.BlockSpec((1, FFN_TN), lambda j: (0, j)),
                  pl.BlockSpec((1, FFN_TN), lambda j: (0, np_ + j)),
                  blk(0), blk(np_), blk(2 * np_), blk(3 * np_)],
        out_specs=[blk(0), blk(0), blk(0)],
        out_shape=[jax.ShapeDtypeStruct((m, D_FF), BF16),
                   jax.ShapeDtypeStruct((m, D_FF), F32),
                   jax.ShapeDtypeStruct((m, D_FF), F32)],
        compiler_params=_params("arbitrary"), name="ffn_up_sample")(
            h, w_up, w_up, conv_w, conv_w, cb, cb, st, st, st, st)
    up = jnp.concatenate([upg, upv], axis=-1)
    new_conv = jnp.concatenate([state[:, 1:], up[:, None, :]], axis=1)
    return act, new_conv


def _layer(x, p, attn_fn, branch_a_fn, ffn_up_fn, tm_big, tm_down, tm_row):
    h = rms_cast(x, p["g_pre_mix"], tm_row)
    z = matmul(h, p["w_in"], tm_big, 512, "in_proj")
    ya_out = branch_a_fn(z, p["a_ln_g"], p["a_ln_b"], p["a_w_s"], p["a_b_s"])
    ya = ya_out[0] if isinstance(ya_out, (list, tuple)) else ya_out
    yb = attn_fn(z)
    mixed = merge(ya, yb, z, p["w_branch_a"], p["w_branch_b"], tm_big, 256)
    t = matmul(mixed, p["w_out"], tm_big, 512, "out_proj")
    x1, h2 = post_mix(x, t, p["g_post_mix"], p["g_pre_ffn"], tm_row)
    act, new_conv = ffn_up_fn(h2)
    t2 = matmul(act, p["w_down"], tm_down, 256, "down_proj")
    y = post_ffn(x1, t2, p["g_post_ffn"], tm_row)
    return y, z, ya_out, new_conv


def kernel(x_prompt, x_sample, cache_k, cache_v, state_ffn_conv, page_table, g_pre_mix, w_in, a_ln_g, a_ln_b, a_w_s, a_b_s, lambda_q1, lambda_k1, lambda_q2, lambda_k2, subln_g, w_branch_a, w_branch_b, w_out, g_post_mix, g_pre_ffn, w_up, conv_w, conv_b, w_down, g_post_ffn):
    l = 0
    p = dict(g_pre_mix=g_pre_mix[l], w_in=w_in[l], a_ln_g=a_ln_g[l], a_ln_b=a_ln_b[l], a_w_s=a_w_s[l],
             a_b_s=a_b_s[l], w_branch_a=w_branch_a[l], w_branch_b=w_branch_b[l], w_out=w_out[l],
             g_post_mix=g_post_mix[l], g_pre_ffn=g_pre_ffn[l], w_up=w_up[l], conv_w=conv_w[l],
             conv_b=conv_b[l], w_down=w_down[l], g_post_ffn=g_post_ffn[l])
    lams = (lambda_q1[l], lambda_k1[l], lambda_q2[l], lambda_k2[l], subln_g[l])

    xp = x_prompt.reshape(SEQ, D_MODEL)
    yp, zp, _, conv_p = _layer(
        xp, p,
        attn_fn=lambda z: attn_prompt(z, SEQ, *lams),
        branch_a_fn=branch_a_prompt,
        ffn_up_fn=lambda h2: ffn_up_prompt(h2, p["w_up"], p["conv_w"], p["conv_b"], 2048),
        tm_big=2048, tm_down=1024, tm_row=256)

    xs = x_sample.reshape(DEC_BATCH, D_MODEL)
    ys, zs, ya_s, conv_s = _layer(
        xs, p,
        attn_fn=lambda z: attn_sample(z, cache_k[l], cache_v[l], page_table, *lams).astype(BF16),
        branch_a_fn=branch_a_sample,
        ffn_up_fn=lambda h2: ffn_up_sample(h2, p["w_up"], p["conv_w"], p["conv_b"], state_ffn_conv[l]),
        tm_big=DEC_BATCH, tm_down=DEC_BATCH, tm_row=DEC_BATCH)

    hs = (N_HEADS, HEAD_WIDTH)
    return (yp.reshape(1, SEQ, D_MODEL),
            ys.reshape(DEC_BATCH, 1, D_MODEL),
            zp[:, COL_K:COL_V].reshape(1, 1, SEQ, *hs),
            zp[:, COL_V:COL_GA].reshape(1, 1, SEQ, *hs),
            zs[:, COL_K:COL_V].reshape(1, DEC_BATCH, 1, *hs),
            zs[:, COL_V:COL_GA].reshape(1, DEC_BATCH, 1, *hs),
            ya_s[1].reshape(1, DEC_BATCH, 1, A_WIDTH),
            conv_p.reshape(1, 1, CONV_W - 1, 2 * D_FF),
            conv_s.reshape(1, DEC_BATCH, CONV_W - 1, 2 * D_FF))
```

```python
import functools
import math

import numpy as np
import jax
import jax.numpy as jnp
from jax import lax
from jax.experimental import pallas as pl
from jax.experimental.pallas import tpu as pltpu

D_MODEL = 4096
SEQ = 8192
DEC_BATCH = 128
PAST_LEN = 2048
PAGE_SIZE = 128
N_PAGES = PAST_LEN // PAGE_SIZE
A_WIDTH = D_MODEL // 2
A_GROUP_WIDTH = 128
A_GROUPS = A_WIDTH // A_GROUP_WIDTH
CHUNK = 128
HEAD_DIM = 128
HEAD_WIDTH = 2 * HEAD_DIM
N_HEADS = D_MODEL // (4 * HEAD_DIM)
B_WIDTH = N_HEADS * HEAD_WIDTH
D_FF = ((8 * D_MODEL // 3 + 255) // 256) * 256
CONV_W = 3
N_IN = 2 * A_WIDTH + 3 * B_WIDTH + 2 * D_MODEL
EPS = 1e-6
NEG_INF = -1e30
LAM_INIT = 0.8 - 0.6 * math.exp(-0.3 * 0)
Q_SCALE = HEAD_DIM ** -0.5
LOG2E = math.log2(math.e)

COL_U = 0
COL_VA = A_WIDTH
COL_Q = 2 * A_WIDTH
COL_K = COL_Q + B_WIDTH
COL_V = COL_K + B_WIDTH
COL_GA = COL_V + B_WIDTH
COL_GB = COL_GA + D_MODEL

V7X_VMEM_LIMIT_BYTES = 58 * 1024 * 1024
FFN_TN = 256
FFN_PANELS = D_FF // FFN_TN
FFN_SUB_ROWS = 512
ATT_T = 512
PAGES_PER_STEP = 8

BF16 = jnp.bfloat16
F32 = jnp.float32


def _params(*sem):
    return pltpu.CompilerParams(dimension_semantics=sem, vmem_limit_bytes=V7X_VMEM_LIMIT_BYTES)


def _gelu(x):
    return 0.5 * x * (1.0 + jnp.tanh(0.7978845608028654 * (x + 0.044715 * (x * x * x))))


def _rms(x, g):
    return x * lax.rsqrt(jnp.mean(x * x, axis=-1, keepdims=True) + EPS) * g


def _rms_cast_kernel(x_ref, g_ref, o_ref):
    o_ref[...] = _rms(x_ref[...], g_ref[...]).astype(o_ref.dtype)


def rms_cast(x, g, tm):
    m, d = x.shape
    return pl.pallas_call(
        _rms_cast_kernel, grid=(m // tm,),
        in_specs=[pl.BlockSpec((tm, d), lambda i: (i, 0)), pl.BlockSpec((1, d), lambda i: (0, 0))],
        out_specs=pl.BlockSpec((tm, d), lambda i: (i, 0)),
        out_shape=jax.ShapeDtypeStruct((m, d), BF16),
        compiler_params=_params("arbitrary"), name="rms_cast")(x, g.reshape(1, d))


def _post_mix_kernel(x_ref, t_ref, g1_ref, g2_ref, x1_ref, h_ref):
    x1 = x_ref[...] + _rms(t_ref[...], g1_ref[...])
    x1_ref[...] = x1
    h_ref[...] = _rms(x1, g2_ref[...]).astype(h_ref.dtype)


def post_mix(x, t, g_post, g_pre_next, tm):
    m, d = x.shape
    row = pl.BlockSpec((tm, d), lambda i: (i, 0))
    vec = pl.BlockSpec((1, d), lambda i: (0, 0))
    return pl.pallas_call(
        _post_mix_kernel, grid=(m // tm,), in_specs=[row, row, vec, vec], out_specs=[row, row],
        out_shape=[jax.ShapeDtypeStruct((m, d), F32), jax.ShapeDtypeStruct((m, d), BF16)],
        compiler_params=_params("arbitrary"), name="post_mix")(
            x, t, g_post.reshape(1, d), g_pre_next.reshape(1, d))


def _post_ffn_kernel(x_ref, t_ref, g_ref, y_ref):
    y_ref[...] = x_ref[...] + _rms(t_ref[...], g_ref[...])


def post_ffn(x, t, g, tm):
    m, d = x.shape
    row = pl.BlockSpec((tm, d), lambda i: (i, 0))
    vec = pl.BlockSpec((1, d), lambda i: (0, 0))
    return pl.pallas_call(
        _post_ffn_kernel, grid=(m // tm,), in_specs=[row, row, vec], out_specs=row,
        out_shape=jax.ShapeDtypeStruct((m, d), F32),
        compiler_params=_params("arbitrary"), name="post_ffn")(x, t, g.reshape(1, d))


def _mm_kernel(a_ref, w_ref, o_ref):
    o_ref[...] = jnp.dot(a_ref[...], w_ref[...].astype(BF16),
                         preferred_element_type=F32).astype(o_ref.dtype)


def matmul(a, w, tm, tn, name):
    m, k = a.shape
    n = w.shape[1]
    return pl.pallas_call(
        _mm_kernel, grid=(m // tm, n // tn),
        in_specs=[pl.BlockSpec((tm, k), lambda i, j: (i, 0), pipeline_mode=pl.Buffered(1)),
                  pl.BlockSpec((k, tn), lambda i, j: (0, j))],
        out_specs=pl.BlockSpec((tm, tn), lambda i, j: (i, j)),
        out_shape=jax.ShapeDtypeStruct((m, n), F32),
        compiler_params=_params("arbitrary", "arbitrary"), name=name)(a, w)


def _layernorm(x, g, b):
    mu = jnp.mean(x, axis=-1, keepdims=True)
    xc = x - mu
    var = jnp.mean(xc * xc, axis=-1, keepdims=True)
    return xc * lax.rsqrt(var + EPS) * g + b


def _branch_a_prompt_kernel(u_ref, va_ref, lng_ref, lnb_ref, ws_ref, bst_ref, ya_ref):
    u = _gelu(u_ref[...])
    vn = _layernorm(_gelu(va_ref[...]), lng_ref[...], lnb_ref[...])
    row = lax.broadcasted_iota(jnp.int32, (CHUNK, CHUNK), 0)
    col = lax.broadcasted_iota(jnp.int32, (CHUNK, CHUNK), 1)
    causal = row >= col
    for g in range(A_GROUPS):
        sl = slice(g * A_GROUP_WIDTH, (g + 1) * A_GROUP_WIDTH)
        w = jnp.where(causal, ws_ref[g], 0.0).astype(BF16)
        mixed = jnp.dot(w, vn[:, sl].astype(BF16), preferred_element_type=F32) + bst_ref[:, g:g + 1]
        ya_ref[:, sl] = (u[:, sl] * mixed).astype(ya_ref.dtype)


def branch_a_prompt(z, ln_g, ln_b, w_s, b_s):
    m = z.shape[0]
    vec = pl.BlockSpec((1, A_WIDTH), lambda i: (0, 0))
    return pl.pallas_call(
        _branch_a_prompt_kernel, grid=(m // CHUNK,),
        in_specs=[pl.BlockSpec((CHUNK, A_WIDTH), lambda i: (i, COL_U // A_WIDTH)),
                  pl.BlockSpec((CHUNK, A_WIDTH), lambda i: (i, COL_VA // A_WIDTH)),
                  vec, vec,
                  pl.BlockSpec((A_GROUPS, CHUNK, CHUNK), lambda i: (0, 0, 0)),
                  pl.BlockSpec((CHUNK, A_GROUPS), lambda i: (0, 0))],
        out_specs=pl.BlockSpec((CHUNK, A_WIDTH), lambda i: (i, 0)),
        out_shape=jax.ShapeDtypeStruct((m, A_WIDTH), BF16),
        compiler_params=_params("arbitrary"), name="branch_a_prompt")(
            z, z, ln_g.reshape(1, A_WIDTH), ln_b.reshape(1, A_WIDTH), w_s, b_s.T)


def _branch_a_sample_kernel(u_ref, va_ref, lng_ref, lnb_ref, w00_ref, b0_ref, ya_ref, vn_ref):
    u = _gelu(u_ref[...])
    vn = _layernorm(_gelu(va_ref[...]), lng_ref[...], lnb_ref[...])
    vn_ref[...] = vn
    ya_ref[...] = (u * (vn * w00_ref[...] + b0_ref[...])).astype(ya_ref.dtype)


def branch_a_sample(z, ln_g, ln_b, w_s, b_s):
    m = z.shape[0]
    w00 = jnp.repeat(w_s[:, 0, 0], A_GROUP_WIDTH).reshape(1, A_WIDTH)
    b0 = jnp.repeat(b_s[:, 0], A_GROUP_WIDTH).reshape(1, A_WIDTH)
    vec = pl.BlockSpec((1, A_WIDTH), lambda i: (0, 0))
    out = pl.BlockSpec((m, A_WIDTH), lambda i: (0, 0))
    return pl.pallas_call(
        _branch_a_sample_kernel, grid=(1,),
        in_specs=[pl.BlockSpec((m, A_WIDTH), lambda i: (0, COL_U // A_WIDTH)),
                  pl.BlockSpec((m, A_WIDTH), lambda i: (0, COL_VA // A_WIDTH)),
                  vec, vec, vec, vec],
        out_specs=[out, out],
        out_shape=[jax.ShapeDtypeStruct((m, A_WIDTH), BF16), jax.ShapeDtypeStruct((m, A_WIDTH), F32)],
        compiler_params=_params("arbitrary"), name="branch_a_sample")(
            z, z, ln_g.reshape(1, A_WIDTH), ln_b.reshape(1, A_WIDTH), w00, b0)


def _lambda(lq1_ref, lk1_ref, lq2_ref, lk2_ref):
    return (jnp.exp(jnp.sum(lq1_ref[...] * lk1_ref[...], axis=1, keepdims=True))
            - jnp.exp(jnp.sum(lq2_ref[...] * lk2_ref[...], axis=1, keepdims=True)) + LAM_INIT)


def _alibi_slopes():
    return np.asarray(2.0 ** (-8.0 * np.arange(1, N_HEADS + 1, dtype=np.float32) / N_HEADS), dtype=np.float32)


N_C_PIECES = 3
EXTRA_COLS = 4 * N_C_PIECES


def _alibi_pieces():
    out = []
    for s in _alibi_slopes().astype(np.float64):
        rest = s * math.log2(math.e)
        for _ in range(N_C_PIECES):
            piece = float(np.asarray(rest, np.float32).astype(BF16).astype(np.float32))
            out.append(piece)
            rest -= piece
    return np.asarray(out, np.float32)


def _split_pos(pos):
    lo = pos % 256
    return lo.astype(F32), (pos - lo).astype(F32)


def _qkv_prep_kernel(c_ref, q_ref, k_ref, v_ref, qt_ref, ka_ref, vt_ref):
    t_tile = ATT_T
    h = pl.program_id(1)
    cs = [c_ref[h * N_C_PIECES + t] for t in range(N_C_PIECES)]

    qt = (q_ref[...] * (Q_SCALE * LOG2E)).T
    row = lax.broadcasted_iota(jnp.int32, (HEAD_DIM, t_tile), 0)
    i_lo, i_hi = _split_pos(lax.broadcasted_iota(jnp.int32, (HEAD_DIM, t_tile), 1))
    qe = jnp.zeros((HEAD_DIM, t_tile), F32)
    for t in range(N_C_PIECES):
        qe = jnp.where(row == 4 * t, i_lo, qe)
        qe = jnp.where(row == 4 * t + 1, i_hi, qe)
        qe = jnp.where((row == 4 * t + 2) | (row == 4 * t + 3), cs[t], qe)
    qt_ref[0] = jnp.concatenate([qt[:HEAD_DIM], qe], axis=0).astype(BF16)
    qt_ref[1] = jnp.concatenate([qt[HEAD_DIM:], qe], axis=0).astype(BF16)

    k = k_ref[...]
    col = lax.broadcasted_iota(jnp.int32, (t_tile, HEAD_DIM), 1)
    j_lo, j_hi = _split_pos(lax.broadcasted_iota(jnp.int32, (t_tile, HEAD_DIM), 0))
    ke = jnp.zeros((t_tile, HEAD_DIM), F32)
    for t in range(N_C_PIECES):
        ke = jnp.where((col == 4 * t) | (col == 4 * t + 1), -cs[t], ke)
        ke = jnp.where(col == 4 * t + 2, j_lo, ke)
        ke = jnp.where(col == 4 * t + 3, j_hi, ke)
    ka_ref[0] = jnp.concatenate([k[:, :HEAD_DIM], ke], axis=1).astype(BF16)
    ka_ref[1] = jnp.concatenate([k[:, HEAD_DIM:], ke], axis=1).astype(BF16)

    vt_ref[...] = v_ref[...].T.astype(BF16)


def qkv_prep(z, seq):
    t_tile = ATT_T
    nt = seq // t_tile
    qb, kb, vb = COL_Q // HEAD_WIDTH, COL_K // HEAD_WIDTH, COL_V // HEAD_WIDTH
    grid_spec = pltpu.PrefetchScalarGridSpec(
        num_scalar_prefetch=1, grid=(nt, N_HEADS),
        in_specs=[pl.BlockSpec((t_tile, HEAD_WIDTH), lambda i, h, c: (i, qb + h)),
                  pl.BlockSpec((t_tile, HEAD_WIDTH), lambda i, h, c: (i, kb + h)),
                  pl.BlockSpec((t_tile, HEAD_WIDTH), lambda i, h, c: (i, vb + h))],
        out_specs=[pl.BlockSpec((None, 2, HEAD_WIDTH, t_tile), lambda i, h, c: (h, 0, 0, i)),
                   pl.BlockSpec((None, 2, None, t_tile, HEAD_WIDTH), lambda i, h, c: (h, 0, i, 0, 0)),
                   pl.BlockSpec((None, None, HEAD_WIDTH, t_tile), lambda i, h, c: (h, i, 0, 0))])
    return pl.pallas_call(
        _qkv_prep_kernel, grid_spec=grid_spec,
        out_shape=[jax.ShapeDtypeStruct((N_HEADS, 2, HEAD_WIDTH, seq), BF16),
                   jax.ShapeDtypeStruct((N_HEADS, 2, nt, t_tile, HEAD_WIDTH), BF16),
                   jax.ShapeDtypeStruct((N_HEADS, nt, HEAD_WIDTH, t_tile), BF16)],
        compiler_params=_params("arbitrary", "arbitrary"), name="qkv_prep")(
            jnp.asarray(_alibi_pieces()), z, z, z)


HEAD_GROUPS = 2


def _attn_pending_chain(n, kv, vt_refs, p_ref, al_ref, acc_ref):
    vt = vt_refs[n // 2][kv]
    acc_ref[n] = acc_ref[n] * al_ref[n] + jnp.dot(vt, p_ref[n], preferred_element_type=F32)


def _attn_scores_chain(n, kj, coff, masked, qt_refs, ka_refs, p_ref, al_ref, m_prev, l_prev):
    t_tile = ATT_T
    g, half = n // 2, n % 2
    s = jnp.dot(ka_refs[g][half, kj], qt_refs[g][half], preferred_element_type=F32)
    if masked:
        key = lax.broadcasted_iota(jnp.int32, (t_tile, t_tile), 0)
        qry = lax.broadcasted_iota(jnp.int32, (t_tile, t_tile), 1)
        s = jnp.where(key > qry, NEG_INF, s)
    m_next = jnp.maximum(m_prev, jnp.max(s, axis=0, keepdims=True) + coff)
    alpha = jnp.exp2(m_prev - m_next)
    p = jnp.exp2(s - (m_next - coff))
    p_ref[n] = p.astype(BF16)
    al_ref[n] = alpha
    return m_next, alpha * l_prev + jnp.sum(p, axis=0, keepdims=True)


SIDE_UNITS = 3
SIDE_GROUPS = 3
SIDE_RING = SIDE_UNITS * SIDE_GROUPS


def _side_copies(unit, pos, pt_ref, ck_ref, cv_ref, kbuf, vbuf, sems):
    page = pt_ref[unit]
    return (pltpu.make_async_copy(ck_ref.at[page], kbuf.at[pos], sems.at[0, pos]),
            pltpu.make_async_copy(cv_ref.at[page], vbuf.at[pos], sems.at[1, pos]))


def _side_scores(unit, pos, state, side):
    (sq_ref, sk_ref, sv_ref, slope_ref, sgr_ref, kbuf, vbuf, m_st, l_st, acc_st, so_ref) = side
    rows = 2 * N_HEADS
    b = unit // N_PAGES
    page_no = unit % N_PAGES
    first = page_no == 0

    q8 = sq_ref[b] * Q_SCALE
    lane = lax.broadcasted_iota(jnp.int32, (N_HEADS, HEAD_WIDTH), 1)
    qmat = jnp.concatenate([jnp.where(lane < HEAD_DIM, q8, 0.0),
                            jnp.where(lane >= HEAD_DIM, q8, 0.0)], axis=0)
    kp = kbuf[pos].astype(BF16)
    s = lax.dot_general(qmat.astype(BF16), kp, (((1,), (1,)), ((), ())), preferred_element_type=F32)
    width = PAGE_SIZE * N_HEADS
    col = lax.broadcasted_iota(jnp.int32, (rows, width), 1)
    row = lax.broadcasted_iota(jnp.int32, (rows, width), 0)
    kpos = page_no * PAGE_SIZE + col // N_HEADS
    bias = -slope_ref[:, :1] * (PAST_LEN - kpos).astype(F32)
    s = jnp.where(col % N_HEADS == row % N_HEADS, s + bias, NEG_INF)

    kn = sk_ref[b]
    vn = sv_ref[b]
    s_new = jnp.sum(qmat * jnp.concatenate([kn, kn], axis=0), axis=1, keepdims=True)
    m_prev = jnp.where(first, jnp.broadcast_to(s_new, m_st.shape), state[0])
    l_prev = jnp.where(first, 1.0, state[1])
    acc_prev = jnp.where(first, jnp.concatenate([vn, vn], axis=0), state[2])

    m_next = jnp.maximum(m_prev, jnp.max(s, axis=1, keepdims=True))
    alpha = jnp.exp(m_prev - m_next)
    p = jnp.exp(s - m_next[:, :1])
    l_next = alpha * l_prev + jnp.sum(p, axis=1, keepdims=True)
    return p.astype(BF16), m_next, l_next, acc_prev * alpha[:, :1]


def _side_values(unit, pos, scored, lam, side):
    (sq_ref, sk_ref, sv_ref, slope_ref, sgr_ref, kbuf, vbuf, m_st, l_st, acc_st, so_ref) = side
    p, m_next, l_next, acc_scaled = scored
    acc = acc_scaled + jnp.dot(p, vbuf[pos].astype(BF16), preferred_element_type=F32)
    o = acc / l_next[:, :1]
    out = o[:N_HEADS] - lam * o[N_HEADS:]
    so_ref[unit // N_PAGES] = _rms(out, sgr_ref[...]) * (1.0 - LAM_INIT)
    return m_next, l_next, acc


def _attn_slot(g, n_slots, kj, kv_pending, coffs, masked, stats, lam, main, dma, side):
    qt_refs, ka_refs, vt_refs, acc_ref, p_ref, al_ref = main
    pt_ref, ck_ref, cv_ref, sems = dma
    kbuf, vbuf, m_st, l_st, acc_st = side[5], side[6], side[7], side[8], side[9]

    def start_slot(slot):
        for k in range(SIDE_UNITS):
            pos = (slot % SIDE_GROUPS) * SIDE_UNITS + k
            for cp in _side_copies(slot * SIDE_UNITS + k, pos, pt_ref, ck_ref, cv_ref, kbuf, vbuf, sems):
                cp.start()

    @pl.when(g == 0)
    def _():
        for slot in range(SIDE_GROUPS - 1):
            start_slot(slot)

    @pl.when(g + SIDE_GROUPS - 1 < n_slots)
    def _():
        start_slot(g + SIDE_GROUPS - 1)

    units = [(g * SIDE_UNITS + k, (g % SIDE_GROUPS) * SIDE_UNITS + k) for k in range(SIDE_UNITS)]
    for unit, pos in units:
        for cp in _side_copies(unit, pos, pt_ref, ck_ref, cv_ref, kbuf, vbuf, sems):
            cp.wait()

    n_chain = 2 * HEAD_GROUPS
    pending = [functools.partial(_attn_pending_chain, n, kv_pending, vt_refs, p_ref, al_ref, acc_ref)
               for n in range(n_chain)]
    new_stats = list(stats)

    def scores(n):
        new_stats[2 * n], new_stats[2 * n + 1] = _attn_scores_chain(
            n, kj, coffs[n // 2], masked, qt_refs, ka_refs, p_ref, al_ref, stats[2 * n], stats[2 * n + 1])

    state = (m_st[...], l_st[...], acc_st[...])
    scored = _side_scores(*units[0], state, side)
    pending[0]()
    pending[1]()
    state = _side_values(*units[0], scored, lam, side)
    scored = _side_scores(*units[1], state, side)
    pending[2]()
    pending[3]()
    state = _side_values(*units[1], scored, lam, side)
    scored = _side_scores(*units[2], state, side)
    scores(0)
    scores(1)
    state = _side_values(*units[2], scored, lam, side)
    scores(2)
    scores(3)
    m_st[...], l_st[...], acc_st[...] = state
    return tuple(new_stats)


def _attn_prompt_kernel(c_ref, pt_ref,
                        qt_a, qt_b, ka_a, ka_b, vt_a, vt_b, lq1_ref, lk1_ref, lq2_ref, lk2_ref, sg_ref,
                        sgr_ref, sq_ref, sk_ref, sv_ref, slope_ref, ck_ref, cv_ref,
                        o_a, o_b, so_ref,
                        acc_ref, p_ref, al_ref, kbuf, vbuf, sems, m_st, l_st, acc_st):
    assert SIDE_UNITS == 3 and HEAD_GROUPS == 2, "_attn_slot writes its interleaving out for these counts"
    t_tile = ATT_T
    o_refs = (o_a, o_b)
    main = ((qt_a, qt_b), (ka_a, ka_b), (vt_a, vt_b), acc_ref, p_ref, al_ref)
    dma = (pt_ref, ck_ref, cv_ref, sems)
    side = (sq_ref, sk_ref, sv_ref, slope_ref, sgr_ref, kbuf, vbuf, m_st, l_st, acc_st, so_ref)
    nt = ka_a.shape[1]
    slots_per_pair = nt * (nt + 1) // 2
    n_slots = pl.num_programs(0) * slots_per_pair
    hp = pl.program_id(0)
    qi = pl.program_id(1)
    g_base = hp * slots_per_pair + (qi * (qi + 1)) // 2
    c_sums = []
    for g in range(HEAD_GROUPS):
        base = (hp + g * (N_HEADS // HEAD_GROUPS)) * N_C_PIECES
        c_sum = c_ref[base]
        for t in range(1, N_C_PIECES):
            c_sum = c_sum + c_ref[base + t]
        c_sums.append(c_sum)
    lam = _lambda(lq1_ref, lk1_ref, lq2_ref, lk2_ref)

    @pl.when((hp == 0) & (qi == 0))
    def _():
        so_ref[...] = jnp.zeros(so_ref.shape, F32)
        m_st[...] = jnp.zeros(m_st.shape, F32)
        l_st[...] = jnp.zeros(l_st.shape, F32)
        acc_st[...] = jnp.zeros(acc_st.shape, F32)

    acc_ref[...] = jnp.zeros(acc_ref.shape, F32)
    p_ref[...] = jnp.zeros(p_ref.shape, BF16)
    al_ref[...] = jnp.ones(al_ref.shape, F32)
    neg = jnp.full((1, t_tile), NEG_INF, F32)
    zero = jnp.zeros((1, t_tile), F32)

    def body(kj, stats):
        dist = ((qi - kj) * t_tile).astype(F32)
        return _attn_slot(g_base + kj, n_slots, kj, jnp.maximum(kj - 1, 0), [-c * dist for c in c_sums], False,
                          stats, lam, main, dma, side)

    stats = lax.fori_loop(0, qi, body, (neg, zero) * (2 * HEAD_GROUPS))
    stats = _attn_slot(g_base + qi, n_slots, qi, jnp.maximum(qi - 1, 0), [0.0] * HEAD_GROUPS, True,
                       stats, lam, main, dma, side)
    for n in range(2 * HEAD_GROUPS):
        _attn_pending_chain(n, qi, main[2], p_ref, al_ref, acc_ref)

    for g in range(HEAD_GROUPS):
        l1, l2 = stats[4 * g + 1], stats[4 * g + 3]
        out = acc_ref[2 * g] / l1 - lam * (acc_ref[2 * g + 1] / l2)
        ms = jnp.mean(out * out, axis=0, keepdims=True)
        y = out * lax.rsqrt(ms + EPS) * sg_ref[...] * (1.0 - LAM_INIT)
        o_refs[g][...] = y.T.astype(o_refs[g].dtype)


def _sample_heads(zs, c0):
    return zs[:, c0:c0 + B_WIDTH].reshape(zs.shape[0], N_HEADS, HEAD_WIDTH)


def _slope_rows():
    return jnp.asarray(np.broadcast_to(np.tile(_alibi_slopes(), 2)[:, None], (2 * N_HEADS, 128)).copy())


def side_sequences(seq):
    nt = seq // ATT_T
    n_units = SIDE_UNITS * (N_HEADS // HEAD_GROUPS) * (nt * (nt + 1) // 2)
    assert n_units % N_PAGES == 0, "the side stream must end on a sequence boundary"
    return n_units // N_PAGES


def attn_prompt(z, seq, zs, cache_k, cache_v, page_table, lq1, lk1, lq2, lk2, subln_g):
    t_tile = ATT_T
    nt = seq // t_tile
    hg = N_HEADS // HEAD_GROUPS
    nb = zs.shape[0]
    assert side_sequences(seq) <= nb
    n_pool = cache_k.shape[0]
    page_rows = PAGE_SIZE * N_HEADS
    ck = cache_k.reshape(n_pool, page_rows, HEAD_WIDTH)
    cv = cache_v.reshape(n_pool, page_rows, HEAD_WIDTH)
    qt, ka, vt = qkv_prep(z, seq)
    const = lambda shape: pl.BlockSpec(shape, lambda h, i, c, pt: (0,) * len(shape))
    whole = lambda shape: pl.BlockSpec(shape, lambda h, i, c, pt: (0,) * len(shape), pipeline_mode=pl.Buffered(1))
    lvec = const((1, HEAD_DIM))
    qspec = lambda g: pl.BlockSpec((None, 2, HEAD_WIDTH, t_tile), lambda h, i, c, pt: (h + g * hg, 0, 0, i))
    kspec = lambda g: pl.BlockSpec((None, 2, nt, t_tile, HEAD_WIDTH), lambda h, i, c, pt: (h + g * hg, 0, 0, 0, 0),
                                   pipeline_mode=pl.Buffered(1))
    vspec = lambda g: pl.BlockSpec((None, nt, HEAD_WIDTH, t_tile), lambda h, i, c, pt: (h + g * hg, 0, 0, 0),
                                   pipeline_mode=pl.Buffered(1))
    ospec = pl.BlockSpec((t_tile, HEAD_WIDTH), lambda h, i, c, pt: (i, h))
    heads = (nb, N_HEADS, HEAD_WIDTH)
    n_chain = 2 * HEAD_GROUPS
    grid_spec = pltpu.PrefetchScalarGridSpec(
        num_scalar_prefetch=2, grid=(hg, nt),
        in_specs=[qspec(0), qspec(1), kspec(0), kspec(1), vspec(0), vspec(1),
                  lvec, lvec, lvec, lvec, const((HEAD_WIDTH, 1)), const((1, HEAD_WIDTH)),
                  whole(heads), whole(heads), whole(heads), const((2 * N_HEADS, 128)),
                  pl.BlockSpec(memory_space=pl.ANY), pl.BlockSpec(memory_space=pl.ANY)],
        out_specs=[ospec, ospec, const(heads)],
        scratch_shapes=[pltpu.VMEM((n_chain, HEAD_WIDTH, t_tile), F32),
                        pltpu.VMEM((n_chain, t_tile, t_tile), BF16),
                        pltpu.VMEM((n_chain, 1, t_tile), F32),
                        pltpu.VMEM((SIDE_RING, page_rows, HEAD_WIDTH), F32),
                        pltpu.VMEM((SIDE_RING, page_rows, HEAD_WIDTH), F32),
                        pltpu.SemaphoreType.DMA((2, SIDE_RING)),
                        pltpu.VMEM((2 * N_HEADS, 128), F32),
                        pltpu.VMEM((2 * N_HEADS, 128), F32),
                        pltpu.VMEM((2 * N_HEADS, HEAD_WIDTH), F32)])
    r = lambda a: a.reshape(1, -1)
    half_out = jax.ShapeDtypeStruct((seq, B_WIDTH // HEAD_GROUPS), BF16)
    y_a, y_b, y_s = pl.pallas_call(
        _attn_prompt_kernel, grid_spec=grid_spec,
        out_shape=[half_out, half_out, jax.ShapeDtypeStruct(heads, F32)],
        compiler_params=_params("arbitrary", "arbitrary"), name="attn_prompt")(
            jnp.asarray(_alibi_pieces()), page_table.reshape(-1),
            qt, qt, ka, ka, vt, vt, r(lq1), r(lk1), r(lq2), r(lk2),
            subln_g.reshape(HEAD_WIDTH, 1), r(subln_g),
            _sample_heads(zs, COL_Q), _sample_heads(zs, COL_K), _sample_heads(zs, COL_V), _slope_rows(), ck, cv)
    return jnp.concatenate([y_a, y_b], axis=1), y_s


def _attn_sample_kernel(pt_ref, q_ref, kn_ref, vn_ref, slope_ref, lq1_ref, lk1_ref, lq2_ref, lk2_ref, sg_ref,
                        *rest):
    npg = PAGES_PER_STEP
    k_refs = rest[:npg]
    v_refs = rest[npg:2 * npg]
    o_ref, qmat_ref, m_ref, l_ref, acc_ref = rest[2 * npg:]
    s_id = pl.program_id(1)
    rows = 2 * N_HEADS

    @pl.when(s_id == 0)
    def _():
        q8 = q_ref[...] * Q_SCALE
        lane = lax.broadcasted_iota(jnp.int32, (N_HEADS, HEAD_WIDTH), 1)
        qmat = jnp.concatenate([jnp.where(lane < HEAD_DIM, q8, 0.0),
                                jnp.where(lane >= HEAD_DIM, q8, 0.0)], axis=0)
        qmat_ref[...] = qmat.astype(BF16)
        kn = kn_ref[...]
        vn = vn_ref[...]
        s_new = jnp.sum(qmat * jnp.concatenate([kn, kn], axis=0), axis=1, keepdims=True)
        m_ref[...] = jnp.broadcast_to(s_new, m_ref.shape)
        l_ref[...] = jnp.ones(l_ref.shape, F32)
        acc_ref[...] = jnp.concatenate([vn, vn], axis=0)

    qmat = qmat_ref[...]
    scores = []
    for i in range(npg):
        kp = k_refs[i][...].astype(BF16)
        scores.append(lax.dot_general(qmat, kp, (((1,), (1,)), ((), ())), preferred_element_type=F32))
    width = npg * PAGE_SIZE * N_HEADS
    col = lax.broadcasted_iota(jnp.int32, (rows, width), 1)
    row = lax.broadcasted_iota(jnp.int32, (rows, width), 0)
    pos = s_id * (npg * PAGE_SIZE) + col // N_HEADS
    bias = -slope_ref[:, :1] * (PAST_LEN - pos).astype(F32)
    s = jnp.where(col % N_HEADS == row % N_HEADS, jnp.concatenate(scores, axis=1) + bias, NEG_INF)
    m_prev = m_ref[...]
    m_next = jnp.maximum(m_prev, jnp.max(s, axis=1, keepdims=True))
    alpha = jnp.exp(m_prev - m_next)
    p = jnp.exp(s - m_next[:, :1])
    l_ref[...] = alpha * l_ref[...] + jnp.sum(p, axis=1, keepdims=True)
    m_ref[...] = m_next
    acc = acc_ref[...] * alpha[:, :1]
    pw = PAGE_SIZE * N_HEADS
    for i in range(npg):
        acc = acc + jnp.dot(p[:, i * pw:(i + 1) * pw].astype(BF16),
                            v_refs[i][...].astype(BF16), preferred_element_type=F32)
    acc_ref[...] = acc

    @pl.when(s_id == pl.num_programs(1) - 1)
    def _():
        lam = _lambda(lq1_ref, lk1_ref, lq2_ref, lk2_ref)
        o = acc_ref[...] / l_ref[...][:, :1]
        out = o[:N_HEADS] - lam * o[N_HEADS:]
        o_ref[...] = _rms(out, sg_ref[...]) * (1.0 - LAM_INIT)


def attn_sample(z, b0, cache_k, cache_v, page_table, lq1, lk1, lq2, lk2, subln_g):
    nb = z.shape[0] - b0
    npg = PAGES_PER_STEP
    n_steps = N_PAGES // npg
    n_pool = cache_k.shape[0]
    ck = cache_k.reshape(n_pool, PAGE_SIZE * N_HEADS, HEAD_WIDTH)
    cv = cache_v.reshape(n_pool, PAGE_SIZE * N_HEADS, HEAD_WIDTH)
    inblk = pl.BlockSpec((None, N_HEADS, HEAD_WIDTH), lambda b, s, pt: (b0 + b, 0, 0))
    outblk = pl.BlockSpec((None, N_HEADS, HEAD_WIDTH), lambda b, s, pt: (b, 0, 0))
    lvec = pl.BlockSpec((1, HEAD_DIM), lambda b, s, pt: (0, 0))

    def page_spec(i):
        return pl.BlockSpec((None, PAGE_SIZE * N_HEADS, HEAD_WIDTH),
                            lambda b, s, pt: (pt[(b0 + b) * N_PAGES + s * npg + i], 0, 0))

    grid_spec = pltpu.PrefetchScalarGridSpec(
        num_scalar_prefetch=1, grid=(nb, n_steps),
        in_specs=[inblk, inblk, inblk,
                  pl.BlockSpec((2 * N_HEADS, 128), lambda b, s, pt: (0, 0)),
                  lvec, lvec, lvec, lvec,
                  pl.BlockSpec((1, HEAD_WIDTH), lambda b, s, pt: (0, 0))]
                 + [page_spec(i) for i in range(npg)] + [page_spec(i) for i in range(npg)],
        out_specs=outblk,
        scratch_shapes=[pltpu.VMEM((2 * N_HEADS, HEAD_WIDTH), BF16),
                        pltpu.VMEM((2 * N_HEADS, 128), F32),
                        pltpu.VMEM((2 * N_HEADS, 128), F32),
                        pltpu.VMEM((2 * N_HEADS, HEAD_WIDTH), F32)])
    r = lambda a: a.reshape(1, -1)
    return pl.pallas_call(
        _attn_sample_kernel, grid_spec=grid_spec,
        out_shape=jax.ShapeDtypeStruct((nb, N_HEADS, HEAD_WIDTH), F32),
        compiler_params=_params("arbitrary", "arbitrary"), name="attn_sample")(
            page_table.reshape(-1), _sample_heads(z, COL_Q), _sample_heads(z, COL_K), _sample_heads(z, COL_V),
            _slope_rows(), r(lq1), r(lk1), r(lq2), r(lk2), r(subln_g), *([ck] * npg), *([cv] * npg))


def _merge_kernel(ya_ref, yb_ref, wa_ref, wb_ref, ga_ref, gb_ref, o_ref):
    a = jnp.dot(ya_ref[...], wa_ref[...].astype(BF16), preferred_element_type=F32)
    b = jnp.dot(yb_ref[...], wb_ref[...].astype(BF16), preferred_element_type=F32)
    o_ref[...] = (jax.nn.sigmoid(ga_ref[...]) * a + jax.nn.sigmoid(gb_ref[...]) * b).astype(o_ref.dtype)


def merge(ya, yb, z, w_a, w_b, tm, tn):
    m = ya.shape[0]
    act = lambda: pl.BlockSpec((tm, A_WIDTH), lambda i, j: (i, 0), pipeline_mode=pl.Buffered(1))
    wsp = pl.BlockSpec((A_WIDTH, tn), lambda i, j: (0, j))
    return pl.pallas_call(
        _merge_kernel, grid=(m // tm, D_MODEL // tn),
        in_specs=[act(), act(), wsp, wsp,
                  pl.BlockSpec((tm, tn), lambda i, j: (i, COL_GA // tn + j)),
                  pl.BlockSpec((tm, tn), lambda i, j: (i, COL_GB // tn + j))],
        out_specs=pl.BlockSpec((tm, tn), lambda i, j: (i, j)),
        out_shape=jax.ShapeDtypeStruct((m, D_MODEL), BF16),
        compiler_params=_params("arbitrary", "arbitrary"), name="merge")(ya, yb, w_a, w_b, z, z)


def _conv3(up, prev1, prev2, cw_ref, cb_ref):
    rows = lax.broadcasted_iota(jnp.int32, up.shape, 0)
    s1 = jnp.where(rows == 0, prev1, pltpu.roll(up, 1, axis=0))
    s2 = jnp.where(rows == 0, prev2, jnp.where(rows == 1, prev1, pltpu.roll(up, 2, axis=0)))
    return cb_ref[...] + cw_ref[0:1, :] * s2 + cw_ref[1:2, :] * s1 + cw_ref[2:3, :] * up


def _ffn_up_prompt_kernel(h_ref, wg_ref, wv_ref, cwg_ref, cwv_ref, cbg_ref, cbv_ref,
                          act_ref, tailg_ref, tailv_ref, carry_ref):
    i = pl.program_id(0)
    j = pl.program_id(1)
    wg = wg_ref[...].astype(BF16)
    wv = wv_ref[...].astype(BF16)

    @pl.when(i == 0)
    def _():
        carry_ref[j] = jnp.zeros(carry_ref.shape[1:], F32)

    prev = carry_ref[j]
    pg1, pg2 = prev[7:8, :FFN_TN], prev[6:7, :FFN_TN]
    pv1, pv2 = prev[7:8, FFN_TN:], prev[6:7, FFN_TN:]
    sub = FFN_SUB_ROWS
    for r in range(h_ref.shape[0] // sub):
        rows = slice(r * sub, (r + 1) * sub)
        h = h_ref[rows, :]
        upg = jnp.dot(h, wg, preferred_element_type=F32)
        upv = jnp.dot(h, wv, preferred_element_type=F32)
        cg = _conv3(upg, pg1, pg2, cwg_ref, cbg_ref)
        cv = _conv3(upv, pv1, pv2, cwv_ref, cbv_ref)
        act_ref[rows, :] = (_gelu(cg) * cv).astype(act_ref.dtype)
        pg1, pg2 = upg[sub - 1:sub, :], upg[sub - 2:sub - 1, :]
        pv1, pv2 = upv[sub - 1:sub, :], upv[sub - 2:sub - 1, :]
    tail_g = upg[sub - 8:, :]
    tail_v = upv[sub - 8:, :]
    carry_ref[j] = jnp.concatenate([tail_g, tail_v], axis=1)
    tailg_ref[...] = tail_g
    tailv_ref[...] = tail_v


def ffn_up_prompt(h, w_up, conv_w, conv_b, tm):
    m = h.shape[0]
    n_mt = m // tm
    np_ = FFN_PANELS
    cb = conv_b.reshape(1, 2 * D_FF)
    tail = pl.BlockSpec((None, 8, FFN_TN), lambda i, j: (i, 0, j))
    act, tail_g, tail_v = pl.pallas_call(
        _ffn_up_prompt_kernel, grid=(n_mt, np_),
        in_specs=[pl.BlockSpec((tm, D_MODEL), lambda i, j: (i, 0), pipeline_mode=pl.Buffered(1)),
                  pl.BlockSpec((D_MODEL, FFN_TN), lambda i, j: (0, j)),
                  pl.BlockSpec((D_MODEL, FFN_TN), lambda i, j: (0, np_ + j)),
                  pl.BlockSpec((CONV_W, FFN_TN), lambda i, j: (0, j)),
                  pl.BlockSpec((CONV_W, FFN_TN), lambda i, j: (0, np_ + j)),
                  pl.BlockSpec((1, FFN_TN), lambda i, j: (0, j)),
                  pl.BlockSpec((1, FFN_TN), lambda i, j: (0, np_ + j))],
        out_specs=[pl.BlockSpec((tm, FFN_TN), lambda i, j: (i, j)), tail, tail],
        out_shape=[jax.ShapeDtypeStruct((m, D_FF), BF16),
                   jax.ShapeDtypeStruct((n_mt, 8, D_FF), F32),
                   jax.ShapeDtypeStruct((n_mt, 8, D_FF), F32)],
        scratch_shapes=[pltpu.VMEM((np_, 8, 2 * FFN_TN), F32)],
        compiler_params=_params("arbitrary", "arbitrary"), name="ffn_up_prompt")(
            h, w_up, w_up, conv_w, conv_w, cb, cb)
    new_conv = jnp.concatenate([tail_g[n_mt - 1, 8 - (CONV_W - 1):], tail_v[n_mt - 1, 8 - (CONV_W - 1):]], axis=-1)
    return act, new_conv


def _ffn_up_sample_kernel(h_ref, wg_ref, wv_ref, cwg_ref, cwv_ref, cbg_ref, cbv_ref,
                          s0g_ref, s0v_ref, s1g_ref, s1v_ref, act_ref, upg_ref, upv_ref):
    h = h_ref[...]
    upg = jnp.dot(h, wg_ref[...].astype(BF16), preferred_element_type=F32)
    upv = jnp.dot(h, wv_ref[...].astype(BF16), preferred_element_type=F32)
    cg = cbg_ref[...] + cwg_ref[0:1, :] * s0g_ref[...] + cwg_ref[1:2, :] * s1g_ref[...] + cwg_ref[2:3, :] * upg
    cv = cbv_ref[...] + cwv_ref[0:1, :] * s0v_ref[...] + cwv_ref[1:2, :] * s1v_ref[...] + cwv_ref[2:3, :] * upv
    act_ref[...] = (_gelu(cg) * cv).astype(act_ref.dtype)
    upg_ref[...] = upg
    upv_ref[...] = upv


def ffn_up_sample(h, w_up, conv_w, conv_b, state):
    m = h.shape[0]
    np_ = FFN_PANELS
    cb = conv_b.reshape(1, 2 * D_FF)
    st = state.reshape(m, (CONV_W - 1) * 2 * D_FF)
    blk = lambda off: pl.BlockSpec((m, FFN_TN), lambda j: (0, off + j))
    act, upg, upv = pl.pallas_call(
        _ffn_up_sample_kernel, grid=(np_,),
        in_specs=[pl.BlockSpec((m, D_MODEL), lambda j: (0, 0)),
                  pl.BlockSpec((D_MODEL, FFN_TN), lambda j: (0, j)),
                  pl.BlockSpec((D_MODEL, FFN_TN), lambda j: (0, np_ + j)),
                  pl.BlockSpec((CONV_W, FFN_TN), lambda j: (0, j)),
                  pl.BlockSpec((CONV_W, FFN_TN), lambda j: (0, np_ + j)),
                  pl.BlockSpec((1, FFN_TN), lambda j: (0, j)),
                  pl.BlockSpec((1, FFN_TN), lambda j: (0, np_ + j)),
                  blk(0), blk(np_), blk(2 * np_), blk(3 * np_)],
        out_specs=[blk(0), blk(0), blk(0)],
        out_shape=[jax.ShapeDtypeStruct((m, D_FF), BF16),
                   jax.ShapeDtypeStruct((m, D_FF), F32),
                   jax.ShapeDtypeStruct((m, D_FF), F32)],
        compiler_params=_params("arbitrary"), name="ffn_up_sample")(
            h, w_up, w_up, conv_w, conv_w, cb, cb, st, st, st, st)
    up = jnp.concatenate([upg, upv], axis=-1)
    new_conv = jnp.concatenate([state[:, 1:], up[:, None, :]], axis=1)
    return act, new_conv


def _in_projection(x, p, tm_big, tm_row):
    h = rms_cast(x, p["g_pre_mix"], tm_row)
    return matmul(h, p["w_in"], tm_big, 512, "in_proj")


def _after_mixers(x, z, ya, yb, p, ffn_up_fn, tm_big, tm_down, tm_row):
    mixed = merge(ya, yb, z, p["w_branch_a"], p["w_branch_b"], tm_big, 256)
    t = matmul(mixed, p["w_out"], tm_big, 512, "out_proj")
    x1, h2 = post_mix(x, t, p["g_post_mix"], p["g_pre_ffn"], tm_row)
    act, new_conv = ffn_up_fn(h2)
    t2 = matmul(act, p["w_down"], tm_down, 256, "down_proj")
    return post_ffn(x1, t2, p["g_post_ffn"], tm_row), new_conv


def kernel(x_prompt, x_sample, cache_k, cache_v, state_ffn_conv, page_table, g_pre_mix, w_in, a_ln_g, a_ln_b, a_w_s, a_b_s, lambda_q1, lambda_k1, lambda_q2, lambda_k2, subln_g, w_branch_a, w_branch_b, w_out, g_post_mix, g_pre_ffn, w_up, conv_w, conv_b, w_down, g_post_ffn):
    l = 0
    p = dict(g_pre_mix=g_pre_mix[l], w_in=w_in[l], a_ln_g=a_ln_g[l], a_ln_b=a_ln_b[l], a_w_s=a_w_s[l],
             a_b_s=a_b_s[l], w_branch_a=w_branch_a[l], w_branch_b=w_branch_b[l], w_out=w_out[l],
             g_post_mix=g_post_mix[l], g_pre_ffn=g_pre_ffn[l], w_up=w_up[l], conv_w=conv_w[l],
             conv_b=conv_b[l], w_down=w_down[l], g_post_ffn=g_post_ffn[l])
    lams = (lambda_q1[l], lambda_k1[l], lambda_q2[l], lambda_k2[l], subln_g[l])
    gate_args = (p["a_ln_g"], p["a_ln_b"], p["a_w_s"], p["a_b_s"])

    xp = x_prompt.reshape(SEQ, D_MODEL)
    xs = x_sample.reshape(DEC_BATCH, D_MODEL)
    zs = _in_projection(xs, p, DEC_BATCH, DEC_BATCH)
    zp = _in_projection(xp, p, 2048, 256)

    n_side = side_sequences(SEQ)
    yb_p, yb_side = attn_prompt(zp, SEQ, zs, cache_k[l], cache_v[l], page_table, *lams)
    yb_rest = attn_sample(zs, n_side, cache_k[l], cache_v[l], page_table, *lams)
    yb_s = jnp.concatenate([yb_side[:n_side], yb_rest], axis=0).reshape(DEC_BATCH, B_WIDTH).astype(BF16)

    ya_p = branch_a_prompt(zp, *gate_args)
    yp, conv_p = _after_mixers(
        xp, zp, ya_p, yb_p, p,
        ffn_up_fn=lambda h2: ffn_up_prompt(h2, p["w_up"], p["conv_w"], p["conv_b"], 2048),
        tm_big=2048, tm_down=1024, tm_row=256)

    ya_s, vn_s = branch_a_sample(zs, *gate_args)
    ys, conv_s = _after_mixers(
        xs, zs, ya_s, yb_s, p,
        ffn_up_fn=lambda h2: ffn_up_sample(h2, p["w_up"], p["conv_w"], p["conv_b"], state_ffn_conv[l]),
        tm_big=DEC_BATCH, tm_down=DEC_BATCH, tm_row=DEC_BATCH)

    hs = (N_HEADS, HEAD_WIDTH)
    return (yp.reshape(1, SEQ, D_MODEL),
            ys.reshape(DEC_BATCH, 1, D_MODEL),
            zp[:, COL_K:COL_V].reshape(1, 1, SEQ, *hs),
            zp[:, COL_V:COL_GA].reshape(1, 1, SEQ, *hs),
            zs[:, COL_K:COL_V].reshape(1, DEC_BATCH, 1, *hs),
            zs[:, COL_V:COL_GA].reshape(1, DEC_BATCH, 1, *hs),
            vn_s.reshape(1, DEC_BATCH, 1, A_WIDTH),
            conv_p.reshape(1, 1, CONV_W - 1, 2 * D_FF),
            conv_s.reshape(1, DEC_BATCH, CONV_W - 1, 2 * D_FF))
```

```python
import functools
import math

import numpy as np
import jax
import jax.numpy as jnp
from jax import lax
from jax.experimental import pallas as pl
from jax.experimental.pallas import tpu as pltpu

D_MODEL = 4096
SEQ = 8192
DEC_BATCH = 128
PAST_LEN = 2048
PAGE_SIZE = 128
N_PAGES = PAST_LEN // PAGE_SIZE
A_WIDTH = D_MODEL // 2
A_GROUP_WIDTH = 128
A_GROUPS = A_WIDTH // A_GROUP_WIDTH
CHUNK = 128
HEAD_DIM = 128
HEAD_WIDTH = 2 * HEAD_DIM
N_HEADS = D_MODEL // (4 * HEAD_DIM)
B_WIDTH = N_HEADS * HEAD_WIDTH
D_FF = ((8 * D_MODEL // 3 + 255) // 256) * 256
CONV_W = 3
N_IN = 2 * A_WIDTH + 3 * B_WIDTH + 2 * D_MODEL
EPS = 1e-6
NEG_INF = -1e30
LAM_INIT = 0.8 - 0.6 * math.exp(-0.3 * 0)
Q_SCALE = HEAD_DIM ** -0.5
LOG2E = math.log2(math.e)

COL_U = 0
COL_VA = A_WIDTH
COL_Q = 2 * A_WIDTH
COL_K = COL_Q + B_WIDTH
COL_V = COL_K + B_WIDTH
COL_GA = COL_V + B_WIDTH
COL_GB = COL_GA + D_MODEL

V7X_VMEM_LIMIT_BYTES = 58 * 1024 * 1024
FFN_TN = 256
FFN_PANELS = D_FF // FFN_TN
ATT_T = 512
PAGES_PER_STEP = 8

BF16 = jnp.bfloat16
F32 = jnp.float32


def _params(*sem):
    return pltpu.CompilerParams(dimension_semantics=sem, vmem_limit_bytes=V7X_VMEM_LIMIT_BYTES)


def _gelu(x):
    return 0.5 * x * (1.0 + jnp.tanh(0.7978845608028654 * (x + 0.044715 * (x * x * x))))


def _rms(x, g):
    return x * lax.rsqrt(jnp.mean(x * x, axis=-1, keepdims=True) + EPS) * g


M_ALL = SEQ + DEC_BATCH
ROW_T = DEC_BATCH
N_PROMPT_TILES = SEQ // ROW_T

_ALL_ROWS = pl.BlockSpec((ROW_T, D_MODEL), lambda i: (i, 0))
_PROMPT_ROWS = pl.BlockSpec((ROW_T, D_MODEL), lambda i: (jnp.minimum(i, N_PROMPT_TILES - 1), 0))
_SAMPLE_ROWS = pl.BlockSpec((ROW_T, D_MODEL), lambda i: (0, 0))
_VEC = pl.BlockSpec((1, D_MODEL), lambda i: (0, 0))


def _group_rows(xp_ref, xs_ref):
    return jnp.where(pl.program_id(0) < N_PROMPT_TILES, xp_ref[...], xs_ref[...])


def _rms_cast_kernel(xp_ref, xs_ref, g_ref, o_ref):
    o_ref[...] = _rms(_group_rows(xp_ref, xs_ref), g_ref[...]).astype(o_ref.dtype)


def rms_cast(xp, xs, g):
    return pl.pallas_call(
        _rms_cast_kernel, grid=(M_ALL // ROW_T,),
        in_specs=[_PROMPT_ROWS, _SAMPLE_ROWS, _VEC], out_specs=_ALL_ROWS,
        out_shape=jax.ShapeDtypeStruct((M_ALL, D_MODEL), BF16),
        compiler_params=_params("arbitrary"), name="rms_cast")(xp, xs, g.reshape(1, D_MODEL))


def _post_mix_kernel(xp_ref, xs_ref, t_ref, g1_ref, g2_ref, x1_ref, h_ref):
    x1 = _group_rows(xp_ref, xs_ref) + _rms(t_ref[...], g1_ref[...])
    x1_ref[...] = x1
    h_ref[...] = _rms(x1, g2_ref[...]).astype(h_ref.dtype)


def post_mix(xp, xs, t, g_post, g_pre_next):
    return pl.pallas_call(
        _post_mix_kernel, grid=(M_ALL // ROW_T,),
        in_specs=[_PROMPT_ROWS, _SAMPLE_ROWS, _ALL_ROWS, _VEC, _VEC], out_specs=[_ALL_ROWS, _ALL_ROWS],
        out_shape=[jax.ShapeDtypeStruct((M_ALL, D_MODEL), F32), jax.ShapeDtypeStruct((M_ALL, D_MODEL), BF16)],
        compiler_params=_params("arbitrary"), name="post_mix")(
            xp, xs, t, g_post.reshape(1, D_MODEL), g_pre_next.reshape(1, D_MODEL))


def _post_ffn_kernel(x_ref, t_ref, g_ref, yp_ref, ys_ref):
    y = x_ref[...] + _rms(t_ref[...], g_ref[...])
    i = pl.program_id(0)

    @pl.when(i < N_PROMPT_TILES)
    def _():
        yp_ref[...] = y

    @pl.when(i == N_PROMPT_TILES)
    def _():
        ys_ref[...] = y


def post_ffn(x, t, g):
    return pl.pallas_call(
        _post_ffn_kernel, grid=(M_ALL // ROW_T,),
        in_specs=[_ALL_ROWS, _ALL_ROWS, _VEC], out_specs=[_PROMPT_ROWS, _SAMPLE_ROWS],
        out_shape=[jax.ShapeDtypeStruct((SEQ, D_MODEL), F32), jax.ShapeDtypeStruct((DEC_BATCH, D_MODEL), F32)],
        compiler_params=_params("arbitrary"), name="post_ffn")(x, t, g.reshape(1, D_MODEL))


def _mm_kernel(a_ref, w_ref, o_ref):
    o_ref[...] = jnp.dot(a_ref[...], w_ref[...].astype(BF16),
                         preferred_element_type=F32).astype(o_ref.dtype)


def matmul(a, w, tm, tn, name):
    m, k = a.shape
    n = w.shape[1]
    return pl.pallas_call(
        _mm_kernel, grid=(m // tm, n // tn),
        in_specs=[pl.BlockSpec((tm, k), lambda i, j: (i, 0), pipeline_mode=pl.Buffered(1)),
                  pl.BlockSpec((k, tn), lambda i, j: (0, j))],
        out_specs=pl.BlockSpec((tm, tn), lambda i, j: (i, j)),
        out_shape=jax.ShapeDtypeStruct((m, n), F32),
        compiler_params=_params("arbitrary", "arbitrary"), name=name)(a, w)


def _layernorm(x, g, b):
    mu = jnp.mean(x, axis=-1, keepdims=True)
    xc = x - mu
    var = jnp.mean(xc * xc, axis=-1, keepdims=True)
    return xc * lax.rsqrt(var + EPS) * g + b


def _branch_a_prompt_kernel(u_ref, va_ref, lng_ref, lnb_ref, ws_ref, bst_ref, ya_ref):
    u = _gelu(u_ref[...])
    vn = _layernorm(_gelu(va_ref[...]), lng_ref[...], lnb_ref[...])
    row = lax.broadcasted_iota(jnp.int32, (CHUNK, CHUNK), 0)
    col = lax.broadcasted_iota(jnp.int32, (CHUNK, CHUNK), 1)
    causal = row >= col
    for g in range(A_GROUPS):
        sl = slice(g * A_GROUP_WIDTH, (g + 1) * A_GROUP_WIDTH)
        w = jnp.where(causal, ws_ref[g], 0.0).astype(BF16)
        mixed = jnp.dot(w, vn[:, sl].astype(BF16), preferred_element_type=F32) + bst_ref[:, g:g + 1]
        ya_ref[:, sl] = (u[:, sl] * mixed).astype(ya_ref.dtype)


def branch_a_prompt(z, m, ln_g, ln_b, w_s, b_s):
    vec = pl.BlockSpec((1, A_WIDTH), lambda i: (0, 0))
    return pl.pallas_call(
        _branch_a_prompt_kernel, grid=(m // CHUNK,),
        in_specs=[pl.BlockSpec((CHUNK, A_WIDTH), lambda i: (i, COL_U // A_WIDTH)),
                  pl.BlockSpec((CHUNK, A_WIDTH), lambda i: (i, COL_VA // A_WIDTH)),
                  vec, vec,
                  pl.BlockSpec((A_GROUPS, CHUNK, CHUNK), lambda i: (0, 0, 0)),
                  pl.BlockSpec((CHUNK, A_GROUPS), lambda i: (0, 0))],
        out_specs=pl.BlockSpec((CHUNK, A_WIDTH), lambda i: (i, 0)),
        out_shape=jax.ShapeDtypeStruct((m, A_WIDTH), BF16),
        compiler_params=_params("arbitrary"), name="branch_a_prompt")(
            z, z, ln_g.reshape(1, A_WIDTH), ln_b.reshape(1, A_WIDTH), w_s, b_s.T)


def _branch_a_sample_kernel(u_ref, va_ref, lng_ref, lnb_ref, w00_ref, b0_ref, ya_ref, vn_ref):
    u = _gelu(u_ref[...])
    vn = _layernorm(_gelu(va_ref[...]), lng_ref[...], lnb_ref[...])
    vn_ref[...] = vn
    ya_ref[...] = (u * (vn * w00_ref[...] + b0_ref[...])).astype(ya_ref.dtype)


def branch_a_sample(z, row0, m, ln_g, ln_b, w_s, b_s):
    assert row0 % m == 0
    w00 = jnp.repeat(w_s[:, 0, 0], A_GROUP_WIDTH).reshape(1, A_WIDTH)
    b0 = jnp.repeat(b_s[:, 0], A_GROUP_WIDTH).reshape(1, A_WIDTH)
    vec = pl.BlockSpec((1, A_WIDTH), lambda i: (0, 0))
    out = pl.BlockSpec((m, A_WIDTH), lambda i: (0, 0))
    return pl.pallas_call(
        _branch_a_sample_kernel, grid=(1,),
        in_specs=[pl.BlockSpec((m, A_WIDTH), lambda i: (row0 // m, COL_U // A_WIDTH)),
                  pl.BlockSpec((m, A_WIDTH), lambda i: (row0 // m, COL_VA // A_WIDTH)),
                  vec, vec, vec, vec],
        out_specs=[out, out],
        out_shape=[jax.ShapeDtypeStruct((m, A_WIDTH), BF16), jax.ShapeDtypeStruct((m, A_WIDTH), F32)],
        compiler_params=_params("arbitrary"), name="branch_a_sample")(
            z, z, ln_g.reshape(1, A_WIDTH), ln_b.reshape(1, A_WIDTH), w00, b0)


def _lambda(lq1_ref, lk1_ref, lq2_ref, lk2_ref):
    return (jnp.exp(jnp.sum(lq1_ref[...] * lk1_ref[...], axis=1, keepdims=True))
            - jnp.exp(jnp.sum(lq2_ref[...] * lk2_ref[...], axis=1, keepdims=True)) + LAM_INIT)


def _alibi_slopes():
    return np.asarray(2.0 ** (-8.0 * np.arange(1, N_HEADS + 1, dtype=np.float32) / N_HEADS), dtype=np.float32)


N_C_PIECES = 3
EXTRA_COLS = 4 * N_C_PIECES


def _alibi_pieces():
    out = []
    for s in _alibi_slopes().astype(np.float64):
        rest = s * math.log2(math.e)
        for _ in range(N_C_PIECES):
            piece = float(np.asarray(rest, np.float32).astype(BF16).astype(np.float32))
            out.append(piece)
            rest -= piece
    return np.asarray(out, np.float32)


def _split_pos(pos):
    lo = pos % 256
    return lo.astype(F32), (pos - lo).astype(F32)


def _qkv_prep_kernel(c_ref, q_ref, k_ref, v_ref, qt_ref, ka_ref, vt_ref):
    t_tile = ATT_T
    h = pl.program_id(1)
    cs = [c_ref[h * N_C_PIECES + t] for t in range(N_C_PIECES)]

    qt = (q_ref[...] * (Q_SCALE * LOG2E)).T
    row = lax.broadcasted_iota(jnp.int32, (HEAD_DIM, t_tile), 0)
    i_lo, i_hi = _split_pos(lax.broadcasted_iota(jnp.int32, (HEAD_DIM, t_tile), 1))
    qe = jnp.zeros((HEAD_DIM, t_tile), F32)
    for t in range(N_C_PIECES):
        qe = jnp.where(row == 4 * t, i_lo, qe)
        qe = jnp.where(row == 4 * t + 1, i_hi, qe)
        qe = jnp.where((row == 4 * t + 2) | (row == 4 * t + 3), cs[t], qe)
    qt_ref[0] = jnp.concatenate([qt[:HEAD_DIM], qe], axis=0).astype(BF16)
    qt_ref[1] = jnp.concatenate([qt[HEAD_DIM:], qe], axis=0).astype(BF16)

    k = k_ref[...]
    col = lax.broadcasted_iota(jnp.int32, (t_tile, HEAD_DIM), 1)
    j_lo, j_hi = _split_pos(lax.broadcasted_iota(jnp.int32, (t_tile, HEAD_DIM), 0))
    ke = jnp.zeros((t_tile, HEAD_DIM), F32)
    for t in range(N_C_PIECES):
        ke = jnp.where((col == 4 * t) | (col == 4 * t + 1), -cs[t], ke)
        ke = jnp.where(col == 4 * t + 2, j_lo, ke)
        ke = jnp.where(col == 4 * t + 3, j_hi, ke)
    ka_ref[0] = jnp.concatenate([k[:, :HEAD_DIM], ke], axis=1).astype(BF16)
    ka_ref[1] = jnp.concatenate([k[:, HEAD_DIM:], ke], axis=1).astype(BF16)

    vt_ref[...] = v_ref[...].T.astype(BF16)


def qkv_prep(z, seq):
    t_tile = ATT_T
    nt = seq // t_tile
    qb, kb, vb = COL_Q // HEAD_WIDTH, COL_K // HEAD_WIDTH, COL_V // HEAD_WIDTH
    grid_spec = pltpu.PrefetchScalarGridSpec(
        num_scalar_prefetch=1, grid=(nt, N_HEADS),
        in_specs=[pl.BlockSpec((t_tile, HEAD_WIDTH), lambda i, h, c: (i, qb + h)),
                  pl.BlockSpec((t_tile, HEAD_WIDTH), lambda i, h, c: (i, kb + h)),
                  pl.BlockSpec((t_tile, HEAD_WIDTH), lambda i, h, c: (i, vb + h))],
        out_specs=[pl.BlockSpec((None, 2, HEAD_WIDTH, t_tile), lambda i, h, c: (h, 0, 0, i)),
                   pl.BlockSpec((None, 2, None, t_tile, HEAD_WIDTH), lambda i, h, c: (h, 0, i, 0, 0)),
                   pl.BlockSpec((None, None, HEAD_WIDTH, t_tile), lambda i, h, c: (h, i, 0, 0))])
    return pl.pallas_call(
        _qkv_prep_kernel, grid_spec=grid_spec,
        out_shape=[jax.ShapeDtypeStruct((N_HEADS, 2, HEAD_WIDTH, seq), BF16),
                   jax.ShapeDtypeStruct((N_HEADS, 2, nt, t_tile, HEAD_WIDTH), BF16),
                   jax.ShapeDtypeStruct((N_HEADS, nt, HEAD_WIDTH, t_tile), BF16)],
        compiler_params=_params("arbitrary", "arbitrary"), name="qkv_prep")(
            jnp.asarray(_alibi_pieces()), z, z, z)


HEAD_GROUPS = 2


def _attn_pending_chain(n, kv, vt_refs, p_ref, al_ref, acc_ref):
    vt = vt_refs[n // 2][kv]
    acc_ref[n] = acc_ref[n] * al_ref[n] + jnp.dot(vt, p_ref[n], preferred_element_type=F32)


def _attn_scores_chain(n, kj, coff, masked, qt_refs, ka_refs, p_ref, al_ref, m_prev, l_prev):
    t_tile = ATT_T
    g, half = n // 2, n % 2
    s = jnp.dot(ka_refs[g][half, kj], qt_refs[g][half], preferred_element_type=F32)
    if masked:
        key = lax.broadcasted_iota(jnp.int32, (t_tile, t_tile), 0)
        qry = lax.broadcasted_iota(jnp.int32, (t_tile, t_tile), 1)
        s = jnp.where(key > qry, NEG_INF, s)
    m_next = jnp.maximum(m_prev, jnp.max(s, axis=0, keepdims=True) + coff)
    alpha = jnp.exp2(m_prev - m_next)
    p = jnp.exp2(s - (m_next - coff))
    p_ref[n] = p.astype(BF16)
    al_ref[n] = alpha
    return m_next, alpha * l_prev + jnp.sum(p, axis=0, keepdims=True)


SIDE_UNITS = 3
SIDE_GROUPS = 3
SIDE_RING = SIDE_UNITS * SIDE_GROUPS


def _side_copies(unit, pos, pt_ref, ck_ref, cv_ref, kbuf, vbuf, sems):
    page = pt_ref[unit]
    return (pltpu.make_async_copy(ck_ref.at[page], kbuf.at[pos], sems.at[0, pos]),
            pltpu.make_async_copy(cv_ref.at[page], vbuf.at[pos], sems.at[1, pos]))


def _side_scores(unit, pos, state, side):
    (sq_ref, sk_ref, sv_ref, slope_ref, sgr_ref, kbuf, vbuf, m_st, l_st, acc_st, so_ref) = side
    rows = 2 * N_HEADS
    b = unit // N_PAGES
    page_no = unit % N_PAGES
    first = page_no == 0

    q8 = sq_ref[b] * Q_SCALE
    lane = lax.broadcasted_iota(jnp.int32, (N_HEADS, HEAD_WIDTH), 1)
    qmat = jnp.concatenate([jnp.where(lane < HEAD_DIM, q8, 0.0),
                            jnp.where(lane >= HEAD_DIM, q8, 0.0)], axis=0)
    kp = kbuf[pos].astype(BF16)
    s = lax.dot_general(qmat.astype(BF16), kp, (((1,), (1,)), ((), ())), preferred_element_type=F32)
    width = PAGE_SIZE * N_HEADS
    col = lax.broadcasted_iota(jnp.int32, (rows, width), 1)
    row = lax.broadcasted_iota(jnp.int32, (rows, width), 0)
    kpos = page_no * PAGE_SIZE + col // N_HEADS
    bias = -slope_ref[:, :1] * (PAST_LEN - kpos).astype(F32)
    s = jnp.where(col % N_HEADS == row % N_HEADS, s + bias, NEG_INF)

    kn = sk_ref[b]
    vn = sv_ref[b]
    s_new = jnp.sum(qmat * jnp.concatenate([kn, kn], axis=0), axis=1, keepdims=True)
    m_prev = jnp.where(first, jnp.broadcast_to(s_new, m_st.shape), state[0])
    l_prev = jnp.where(first, 1.0, state[1])
    acc_prev = jnp.where(first, jnp.concatenate([vn, vn], axis=0), state[2])

    m_next = jnp.maximum(m_prev, jnp.max(s, axis=1, keepdims=True))
    alpha = jnp.exp(m_prev - m_next)
    p = jnp.exp(s - m_next[:, :1])
    l_next = alpha * l_prev + jnp.sum(p, axis=1, keepdims=True)
    return p.astype(BF16), m_next, l_next, acc_prev * alpha[:, :1]


def _side_values(unit, pos, scored, lam, side):
    (sq_ref, sk_ref, sv_ref, slope_ref, sgr_ref, kbuf, vbuf, m_st, l_st, acc_st, so_ref) = side
    p, m_next, l_next, acc_scaled = scored
    acc = acc_scaled + jnp.dot(p, vbuf[pos].astype(BF16), preferred_element_type=F32)
    o = acc / l_next[:, :1]
    out = o[:N_HEADS] - lam * o[N_HEADS:]
    so_ref[unit // N_PAGES] = _rms(out, sgr_ref[...]) * (1.0 - LAM_INIT)
    return m_next, l_next, acc


def _attn_slot(g, n_slots, kj, kv_pending, coffs, masked, stats, lam, main, dma, side):
    qt_refs, ka_refs, vt_refs, acc_ref, p_ref, al_ref = main
    pt_ref, ck_ref, cv_ref, sems = dma
    kbuf, vbuf, m_st, l_st, acc_st = side[5], side[6], side[7], side[8], side[9]

    def start_slot(slot):
        for k in range(SIDE_UNITS):
            pos = (slot % SIDE_GROUPS) * SIDE_UNITS + k
            for cp in _side_copies(slot * SIDE_UNITS + k, pos, pt_ref, ck_ref, cv_ref, kbuf, vbuf, sems):
                cp.start()

    @pl.when(g == 0)
    def _():
        for slot in range(SIDE_GROUPS - 1):
            start_slot(slot)

    @pl.when(g + SIDE_GROUPS - 1 < n_slots)
    def _():
        start_slot(g + SIDE_GROUPS - 1)

    units = [(g * SIDE_UNITS + k, (g % SIDE_GROUPS) * SIDE_UNITS + k) for k in range(SIDE_UNITS)]
    for unit, pos in units:
        for cp in _side_copies(unit, pos, pt_ref, ck_ref, cv_ref, kbuf, vbuf, sems):
            cp.wait()

    n_chain = 2 * HEAD_GROUPS
    pending = [functools.partial(_attn_pending_chain, n, kv_pending, vt_refs, p_ref, al_ref, acc_ref)
               for n in range(n_chain)]
    new_stats = list(stats)

    def scores(n):
        new_stats[2 * n], new_stats[2 * n + 1] = _attn_scores_chain(
            n, kj, coffs[n // 2], masked, qt_refs, ka_refs, p_ref, al_ref, stats[2 * n], stats[2 * n + 1])

    state = (m_st[...], l_st[...], acc_st[...])
    scored = _side_scores(*units[0], state, side)
    pending[0]()
    pending[1]()
    state = _side_values(*units[0], scored, lam, side)
    scored = _side_scores(*units[1], state, side)
    pending[2]()
    pending[3]()
    state = _side_values(*units[1], scored, lam, side)
    scored = _side_scores(*units[2], state, side)
    scores(0)
    scores(1)
    state = _side_values(*units[2], scored, lam, side)
    scores(2)
    scores(3)
    m_st[...], l_st[...], acc_st[...] = state
    return tuple(new_stats)


def _attn_prompt_kernel(c_ref, pt_ref,
                        qt_a, qt_b, ka_a, ka_b, vt_a, vt_b, lq1_ref, lk1_ref, lq2_ref, lk2_ref, sg_ref,
                        sgr_ref, sq_ref, sk_ref, sv_ref, slope_ref, ck_ref, cv_ref,
                        o_a, o_b, so_ref,
                        acc_ref, p_ref, al_ref, kbuf, vbuf, sems, m_st, l_st, acc_st):
    assert SIDE_UNITS == 3 and HEAD_GROUPS == 2, "_attn_slot writes its interleaving out for these counts"
    t_tile = ATT_T
    o_refs = (o_a, o_b)
    main = ((qt_a, qt_b), (ka_a, ka_b), (vt_a, vt_b), acc_ref, p_ref, al_ref)
    dma = (pt_ref, ck_ref, cv_ref, sems)
    side = (sq_ref, sk_ref, sv_ref, slope_ref, sgr_ref, kbuf, vbuf, m_st, l_st, acc_st, so_ref)
    nt = ka_a.shape[1]
    slots_per_pair = nt * (nt + 1) // 2
    n_slots = pl.num_programs(0) * slots_per_pair
    hp = pl.program_id(0)
    qi = pl.program_id(1)
    g_base = hp * slots_per_pair + (qi * (qi + 1)) // 2
    c_sums = []
    for g in range(HEAD_GROUPS):
        base = (hp + g * (N_HEADS // HEAD_GROUPS)) * N_C_PIECES
        c_sum = c_ref[base]
        for t in range(1, N_C_PIECES):
            c_sum = c_sum + c_ref[base + t]
        c_sums.append(c_sum)
    lam = _lambda(lq1_ref, lk1_ref, lq2_ref, lk2_ref)

    @pl.when((hp == 0) & (qi == 0))
    def _():
        so_ref[...] = jnp.zeros(so_ref.shape, F32)
        m_st[...] = jnp.zeros(m_st.shape, F32)
        l_st[...] = jnp.zeros(l_st.shape, F32)
        acc_st[...] = jnp.zeros(acc_st.shape, F32)

    acc_ref[...] = jnp.zeros(acc_ref.shape, F32)
    p_ref[...] = jnp.zeros(p_ref.shape, BF16)
    al_ref[...] = jnp.ones(al_ref.shape, F32)
    neg = jnp.full((1, t_tile), NEG_INF, F32)
    zero = jnp.zeros((1, t_tile), F32)

    def body(kj, stats):
        dist = ((qi - kj) * t_tile).astype(F32)
        return _attn_slot(g_base + kj, n_slots, kj, jnp.maximum(kj - 1, 0), [-c * dist for c in c_sums], False,
                          stats, lam, main, dma, side)

    stats = lax.fori_loop(0, qi, body, (neg, zero) * (2 * HEAD_GROUPS))
    stats = _attn_slot(g_base + qi, n_slots, qi, jnp.maximum(qi - 1, 0), [0.0] * HEAD_GROUPS, True,
                       stats, lam, main, dma, side)
    for n in range(2 * HEAD_GROUPS):
        _attn_pending_chain(n, qi, main[2], p_ref, al_ref, acc_ref)

    for g in range(HEAD_GROUPS):
        l1, l2 = stats[4 * g + 1], stats[4 * g + 3]
        out = acc_ref[2 * g] / l1 - lam * (acc_ref[2 * g + 1] / l2)
        ms = jnp.mean(out * out, axis=0, keepdims=True)
        y = out * lax.rsqrt(ms + EPS) * sg_ref[...] * (1.0 - LAM_INIT)
        o_refs[g][...] = y.T.astype(o_refs[g].dtype)


def _sample_heads(zs, c0):
    return zs[:, c0:c0 + B_WIDTH].reshape(zs.shape[0], N_HEADS, HEAD_WIDTH)


def _slope_rows():
    return jnp.asarray(np.broadcast_to(np.tile(_alibi_slopes(), 2)[:, None], (2 * N_HEADS, 128)).copy())


def side_sequences(seq):
    nt = seq // ATT_T
    n_units = SIDE_UNITS * (N_HEADS // HEAD_GROUPS) * (nt * (nt + 1) // 2)
    assert n_units % N_PAGES == 0, "the side stream must end on a sequence boundary"
    return n_units // N_PAGES


def attn_prompt(z, seq, zs, cache_k, cache_v, page_table, lq1, lk1, lq2, lk2, subln_g):
    t_tile = ATT_T
    nt = seq // t_tile
    hg = N_HEADS // HEAD_GROUPS
    nb = zs.shape[0]
    assert side_sequences(seq) <= nb
    n_pool = cache_k.shape[0]
    page_rows = PAGE_SIZE * N_HEADS
    ck = cache_k.reshape(n_pool, page_rows, HEAD_WIDTH)
    cv = cache_v.reshape(n_pool, page_rows, HEAD_WIDTH)
    qt, ka, vt = qkv_prep(z, seq)
    const = lambda shape: pl.BlockSpec(shape, lambda h, i, c, pt: (0,) * len(shape))
    whole = lambda shape: pl.BlockSpec(shape, lambda h, i, c, pt: (0,) * len(shape), pipeline_mode=pl.Buffered(1))
    lvec = const((1, HEAD_DIM))
    qspec = lambda g: pl.BlockSpec((None, 2, HEAD_WIDTH, t_tile), lambda h, i, c, pt: (h + g * hg, 0, 0, i))
    kspec = lambda g: pl.BlockSpec((None, 2, nt, t_tile, HEAD_WIDTH), lambda h, i, c, pt: (h + g * hg, 0, 0, 0, 0),
                                   pipeline_mode=pl.Buffered(1))
    vspec = lambda g: pl.BlockSpec((None, nt, HEAD_WIDTH, t_tile), lambda h, i, c, pt: (h + g * hg, 0, 0, 0),
                                   pipeline_mode=pl.Buffered(1))
    ospec = pl.BlockSpec((t_tile, HEAD_WIDTH), lambda h, i, c, pt: (i, h))
    heads = (nb, N_HEADS, HEAD_WIDTH)
    n_chain = 2 * HEAD_GROUPS
    grid_spec = pltpu.PrefetchScalarGridSpec(
        num_scalar_prefetch=2, grid=(hg, nt),
        in_specs=[qspec(0), qspec(1), kspec(0), kspec(1), vspec(0), vspec(1),
                  lvec, lvec, lvec, lvec, const((HEAD_WIDTH, 1)), const((1, HEAD_WIDTH)),
                  whole(heads), whole(heads), whole(heads), const((2 * N_HEADS, 128)),
                  pl.BlockSpec(memory_space=pl.ANY), pl.BlockSpec(memory_space=pl.ANY)],
        out_specs=[ospec, ospec, const(heads)],
        scratch_shapes=[pltpu.VMEM((n_chain, HEAD_WIDTH, t_tile), F32),
                        pltpu.VMEM((n_chain, t_tile, t_tile), BF16),
                        pltpu.VMEM((n_chain, 1, t_tile), F32),
                        pltpu.VMEM((SIDE_RING, page_rows, HEAD_WIDTH), F32),
                        pltpu.VMEM((SIDE_RING, page_rows, HEAD_WIDTH), F32),
                        pltpu.SemaphoreType.DMA((2, SIDE_RING)),
                        pltpu.VMEM((2 * N_HEADS, 128), F32),
                        pltpu.VMEM((2 * N_HEADS, 128), F32),
                        pltpu.VMEM((2 * N_HEADS, HEAD_WIDTH), F32)])
    r = lambda a: a.reshape(1, -1)
    half_out = jax.ShapeDtypeStruct((seq, B_WIDTH // HEAD_GROUPS), BF16)
    y_a, y_b, y_s = pl.pallas_call(
        _attn_prompt_kernel, grid_spec=grid_spec,
        out_shape=[half_out, half_out, jax.ShapeDtypeStruct(heads, F32)],
        compiler_params=_params("arbitrary", "arbitrary"), name="attn_prompt")(
            jnp.asarray(_alibi_pieces()), page_table.reshape(-1),
            qt, qt, ka, ka, vt, vt, r(lq1), r(lk1), r(lq2), r(lk2),
            subln_g.reshape(HEAD_WIDTH, 1), r(subln_g),
            _sample_heads(zs, COL_Q), _sample_heads(zs, COL_K), _sample_heads(zs, COL_V), _slope_rows(), ck, cv)
    return jnp.concatenate([y_a, y_b], axis=1), y_s


def _attn_sample_kernel(pt_ref, q_ref, kn_ref, vn_ref, slope_ref, lq1_ref, lk1_ref, lq2_ref, lk2_ref, sg_ref,
                        *rest):
    npg = PAGES_PER_STEP
    k_refs = rest[:npg]
    v_refs = rest[npg:2 * npg]
    o_ref, qmat_ref, m_ref, l_ref, acc_ref = rest[2 * npg:]
    s_id = pl.program_id(1)
    rows = 2 * N_HEADS

    @pl.when(s_id == 0)
    def _():
        q8 = q_ref[...] * Q_SCALE
        lane = lax.broadcasted_iota(jnp.int32, (N_HEADS, HEAD_WIDTH), 1)
        qmat = jnp.concatenate([jnp.where(lane < HEAD_DIM, q8, 0.0),
                                jnp.where(lane >= HEAD_DIM, q8, 0.0)], axis=0)
        qmat_ref[...] = qmat.astype(BF16)
        kn = kn_ref[...]
        vn = vn_ref[...]
        s_new = jnp.sum(qmat * jnp.concatenate([kn, kn], axis=0), axis=1, keepdims=True)
        m_ref[...] = jnp.broadcast_to(s_new, m_ref.shape)
        l_ref[...] = jnp.ones(l_ref.shape, F32)
        acc_ref[...] = jnp.concatenate([vn, vn], axis=0)

    qmat = qmat_ref[...]
    scores = []
    for i in range(npg):
        kp = k_refs[i][...].astype(BF16)
        scores.append(lax.dot_general(qmat, kp, (((1,), (1,)), ((), ())), preferred_element_type=F32))
    width = npg * PAGE_SIZE * N_HEADS
    col = lax.broadcasted_iota(jnp.int32, (rows, width), 1)
    row = lax.broadcasted_iota(jnp.int32, (rows, width), 0)
    pos = s_id * (npg * PAGE_SIZE) + col // N_HEADS
    bias = -slope_ref[:, :1] * (PAST_LEN - pos).astype(F32)
    s = jnp.where(col % N_HEADS == row % N_HEADS, jnp.concatenate(scores, axis=1) + bias, NEG_INF)
    m_prev = m_ref[...]
    m_next = jnp.maximum(m_prev, jnp.max(s, axis=1, keepdims=True))
    alpha = jnp.exp(m_prev - m_next)
    p = jnp.exp(s - m_next[:, :1])
    l_ref[...] = alpha * l_ref[...] + jnp.sum(p, axis=1, keepdims=True)
    m_ref[...] = m_next
    acc = acc_ref[...] * alpha[:, :1]
    pw = PAGE_SIZE * N_HEADS
    for i in range(npg):
        acc = acc + jnp.dot(p[:, i * pw:(i + 1) * pw].astype(BF16),
                            v_refs[i][...].astype(BF16), preferred_element_type=F32)
    acc_ref[...] = acc

    @pl.when(s_id == pl.num_programs(1) - 1)
    def _():
        lam = _lambda(lq1_ref, lk1_ref, lq2_ref, lk2_ref)
        o = acc_ref[...] / l_ref[...][:, :1]
        out = o[:N_HEADS] - lam * o[N_HEADS:]
        o_ref[...] = _rms(out, sg_ref[...]) * (1.0 - LAM_INIT)


def attn_sample(z, b0, cache_k, cache_v, page_table, lq1, lk1, lq2, lk2, subln_g):
    nb = z.shape[0] - b0
    npg = PAGES_PER_STEP
    n_steps = N_PAGES // npg
    n_pool = cache_k.shape[0]
    ck = cache_k.reshape(n_pool, PAGE_SIZE * N_HEADS, HEAD_WIDTH)
    cv = cache_v.reshape(n_pool, PAGE_SIZE * N_HEADS, HEAD_WIDTH)
    inblk = pl.BlockSpec((None, N_HEADS, HEAD_WIDTH), lambda b, s, pt: (b0 + b, 0, 0))
    outblk = pl.BlockSpec((None, N_HEADS, HEAD_WIDTH), lambda b, s, pt: (b, 0, 0))
    lvec = pl.BlockSpec((1, HEAD_DIM), lambda b, s, pt: (0, 0))

    def page_spec(i):
        return pl.BlockSpec((None, PAGE_SIZE * N_HEADS, HEAD_WIDTH),
                            lambda b, s, pt: (pt[(b0 + b) * N_PAGES + s * npg + i], 0, 0))

    grid_spec = pltpu.PrefetchScalarGridSpec(
        num_scalar_prefetch=1, grid=(nb, n_steps),
        in_specs=[inblk, inblk, inblk,
                  pl.BlockSpec((2 * N_HEADS, 128), lambda b, s, pt: (0, 0)),
                  lvec, lvec, lvec, lvec,
                  pl.BlockSpec((1, HEAD_WIDTH), lambda b, s, pt: (0, 0))]
                 + [page_spec(i) for i in range(npg)] + [page_spec(i) for i in range(npg)],
        out_specs=outblk,
        scratch_shapes=[pltpu.VMEM((2 * N_HEADS, HEAD_WIDTH), BF16),
                        pltpu.VMEM((2 * N_HEADS, 128), F32),
                        pltpu.VMEM((2 * N_HEADS, 128), F32),
                        pltpu.VMEM((2 * N_HEADS, HEAD_WIDTH), F32)])
    r = lambda a: a.reshape(1, -1)
    return pl.pallas_call(
        _attn_sample_kernel, grid_spec=grid_spec,
        out_shape=jax.ShapeDtypeStruct((nb, N_HEADS, HEAD_WIDTH), F32),
        compiler_params=_params("arbitrary", "arbitrary"), name="attn_sample")(
            page_table.reshape(-1), _sample_heads(z, COL_Q), _sample_heads(z, COL_K), _sample_heads(z, COL_V),
            _slope_rows(), r(lq1), r(lk1), r(lq2), r(lk2), r(subln_g), *([ck] * npg), *([cv] * npg))


def _merge_kernel(ya_ref, yb_ref, wa_ref, wb_ref, ga_ref, gb_ref, o_ref):
    a = jnp.dot(ya_ref[...], wa_ref[...].astype(BF16), preferred_element_type=F32)
    b = jnp.dot(yb_ref[...], wb_ref[...].astype(BF16), preferred_element_type=F32)
    o_ref[...] = (jax.nn.sigmoid(ga_ref[...]) * a + jax.nn.sigmoid(gb_ref[...]) * b).astype(o_ref.dtype)


def merge(ya, yb, z, w_a, w_b, tm, tn):
    m = ya.shape[0]
    act = lambda: pl.BlockSpec((tm, A_WIDTH), lambda i, j: (i, 0), pipeline_mode=pl.Buffered(1))
    wsp = pl.BlockSpec((A_WIDTH, tn), lambda i, j: (0, j))
    return pl.pallas_call(
        _merge_kernel, grid=(m // tm, D_MODEL // tn),
        in_specs=[act(), act(), wsp, wsp,
                  pl.BlockSpec((tm, tn), lambda i, j: (i, COL_GA // tn + j)),
                  pl.BlockSpec((tm, tn), lambda i, j: (i, COL_GB // tn + j))],
        out_specs=pl.BlockSpec((tm, tn), lambda i, j: (i, j)),
        out_shape=jax.ShapeDtypeStruct((m, D_MODEL), BF16),
        compiler_params=_params("arbitrary", "arbitrary"), name="merge")(ya, yb, w_a, w_b, z, z)


def _conv3(up, prev1, prev2, cw_ref, cb_ref, own_state=None):
    rows = lax.broadcasted_iota(jnp.int32, up.shape, 0)
    s1 = jnp.where(rows == 0, prev1, pltpu.roll(up, 1, axis=0))
    s2 = jnp.where(rows == 0, prev2, jnp.where(rows == 1, prev1, pltpu.roll(up, 2, axis=0)))
    if own_state is not None:
        mask, st1, st0 = own_state
        s1 = jnp.where(mask, st1, s1)
        s2 = jnp.where(mask, st0, s2)
    return cb_ref[...] + cw_ref[0:1, :] * s2 + cw_ref[1:2, :] * s1 + cw_ref[2:3, :] * up


def _ffn_up_kernel(h_ref, wg_ref, wv_ref, cwg_ref, cwv_ref, cbg_ref, cbv_ref,
                   s0g_ref, s0v_ref, s1g_ref, s1v_ref,
                   act_ref, tailg_ref, tailv_ref, upsg_ref, upsv_ref, carry_ref, *, sub):
    i = pl.program_id(0)
    j = pl.program_id(1)
    n_samp = s0g_ref.shape[0]
    n_sub = h_ref.shape[0] // sub
    edge = sub - n_samp
    wg = wg_ref[...].astype(BF16)
    wv = wv_ref[...].astype(BF16)

    @pl.when(i == 0)
    def _():
        carry_ref[j] = jnp.zeros(carry_ref.shape[1:], F32)

    prev = carry_ref[j]
    pg1, pg2 = prev[7:8, :FFN_TN], prev[6:7, :FFN_TN]
    pv1, pv2 = prev[7:8, FFN_TN:], prev[6:7, FFN_TN:]
    for r in range(n_sub):
        rows = slice(r * sub, (r + 1) * sub)
        h = h_ref[rows, :]
        upg = jnp.dot(h, wg, preferred_element_type=F32)
        upv = jnp.dot(h, wv, preferred_element_type=F32)
        own_g = own_v = None
        if r == n_sub - 1:
            local = lax.broadcasted_iota(jnp.int32, (sub, FFN_TN), 0)
            mask = (local >= edge) & (i == pl.num_programs(0) - 1)
            pad = lambda ref: jnp.concatenate([jnp.zeros((edge, FFN_TN), F32), ref[...]], axis=0)
            own_g = (mask, pad(s1g_ref), pad(s0g_ref))
            own_v = (mask, pad(s1v_ref), pad(s0v_ref))
        cg = _conv3(upg, pg1, pg2, cwg_ref, cbg_ref, own_g)
        cv = _conv3(upv, pv1, pv2, cwv_ref, cbv_ref, own_v)
        act_ref[rows, :] = (_gelu(cg) * cv).astype(act_ref.dtype)
        pg1, pg2 = upg[sub - 1:sub, :], upg[sub - 2:sub - 1, :]
        pv1, pv2 = upv[sub - 1:sub, :], upv[sub - 2:sub - 1, :]
    carry_ref[j] = jnp.concatenate([upg[sub - 8:, :], upv[sub - 8:, :]], axis=1)
    tailg_ref[...] = upg[edge - 8:edge, :]
    tailv_ref[...] = upv[edge - 8:edge, :]
    upsg_ref[...] = upg[edge:, :]
    upsv_ref[...] = upv[edge:, :]


def ffn_up(h, w_up, conv_w, conv_b, state, tm, sub):
    m = h.shape[0]
    n_mt = m // tm
    n_samp = state.shape[0]
    assert m % tm == 0 and tm % sub == 0 and sub > n_samp + 8
    np_ = FFN_PANELS
    cb = conv_b.reshape(1, 2 * D_FF)
    st = state.reshape(n_samp, (CONV_W - 1) * 2 * D_FF)
    sblk = lambda off: pl.BlockSpec((n_samp, FFN_TN), lambda i, j: (0, off + j))
    tail = pl.BlockSpec((None, 8, FFN_TN), lambda i, j: (i, 0, j))
    ups = pl.BlockSpec((None, n_samp, FFN_TN), lambda i, j: (i, 0, j))
    act, tail_g, tail_v, ups_g, ups_v = pl.pallas_call(
        functools.partial(_ffn_up_kernel, sub=sub), grid=(n_mt, np_),
        in_specs=[pl.BlockSpec((tm, D_MODEL), lambda i, j: (i, 0), pipeline_mode=pl.Buffered(1)),
                  pl.BlockSpec((D_MODEL, FFN_TN), lambda i, j: (0, j)),
                  pl.BlockSpec((D_MODEL, FFN_TN), lambda i, j: (0, np_ + j)),
                  pl.BlockSpec((CONV_W, FFN_TN), lambda i, j: (0, j)),
                  pl.BlockSpec((CONV_W, FFN_TN), lambda i, j: (0, np_ + j)),
                  pl.BlockSpec((1, FFN_TN), lambda i, j: (0, j)),
                  pl.BlockSpec((1, FFN_TN), lambda i, j: (0, np_ + j)),
                  sblk(0), sblk(np_), sblk(2 * np_), sblk(3 * np_)],
        out_specs=[pl.BlockSpec((tm, FFN_TN), lambda i, j: (i, j)), tail, tail, ups, ups],
        out_shape=[jax.ShapeDtypeStruct((m, D_FF), BF16),
                   jax.ShapeDtypeStruct((n_mt, 8, D_FF), F32),
                   jax.ShapeDtypeStruct((n_mt, 8, D_FF), F32),
                   jax.ShapeDtypeStruct((n_mt, n_samp, D_FF), F32),
                   jax.ShapeDtypeStruct((n_mt, n_samp, D_FF), F32)],
        scratch_shapes=[pltpu.VMEM((np_, 8, 2 * FFN_TN), F32)],
        compiler_params=_params("arbitrary", "arbitrary"), name="ffn_up")(
            h, w_up, w_up, conv_w, conv_w, cb, cb, st, st, st, st)
    keep = 8 - (CONV_W - 1)
    conv_prompt = jnp.concatenate([tail_g[n_mt - 1, keep:], tail_v[n_mt - 1, keep:]], axis=-1)
    up_s = jnp.concatenate([ups_g[n_mt - 1], ups_v[n_mt - 1]], axis=-1)
    conv_sample = jnp.concatenate([state[:, 1:], up_s[:, None, :]], axis=1)
    return act, conv_prompt, conv_sample


TM_BIG = M_ALL // 4
TM_DOWN = M_ALL // 8
FFN_SUB = TM_BIG // 5


def kernel(x_prompt, x_sample, cache_k, cache_v, state_ffn_conv, page_table, g_pre_mix, w_in, a_ln_g, a_ln_b, a_w_s, a_b_s, lambda_q1, lambda_k1, lambda_q2, lambda_k2, subln_g, w_branch_a, w_branch_b, w_out, g_post_mix, g_pre_ffn, w_up, conv_w, conv_b, w_down, g_post_ffn):
    l = 0
    p = dict(g_pre_mix=g_pre_mix[l], w_in=w_in[l], a_ln_g=a_ln_g[l], a_ln_b=a_ln_b[l], a_w_s=a_w_s[l],
             a_b_s=a_b_s[l], w_branch_a=w_branch_a[l], w_branch_b=w_branch_b[l], w_out=w_out[l],
             g_post_mix=g_post_mix[l], g_pre_ffn=g_pre_ffn[l], w_up=w_up[l], conv_w=conv_w[l],
             conv_b=conv_b[l], w_down=w_down[l], g_post_ffn=g_post_ffn[l])
    lams = (lambda_q1[l], lambda_k1[l], lambda_q2[l], lambda_k2[l], subln_g[l])
    gate_args = (p["a_ln_g"], p["a_ln_b"], p["a_w_s"], p["a_b_s"])

    xp = x_prompt.reshape(SEQ, D_MODEL)
    xs = x_sample.reshape(DEC_BATCH, D_MODEL)
    h = rms_cast(xp, xs, p["g_pre_mix"])
    z = matmul(h, p["w_in"], TM_BIG, 512, "in_proj")
    zs = z[SEQ:]

    n_side = side_sequences(SEQ)
    yb_p, yb_side = attn_prompt(z, SEQ, zs, cache_k[l], cache_v[l], page_table, *lams)
    yb_rest = attn_sample(zs, n_side, cache_k[l], cache_v[l], page_table, *lams)
    yb_s = jnp.concatenate([yb_side[:n_side], yb_rest], axis=0).reshape(DEC_BATCH, B_WIDTH).astype(BF16)
    ya_p = branch_a_prompt(z, SEQ, *gate_args)
    ya_s, vn_s = branch_a_sample(z, SEQ, DEC_BATCH, *gate_args)

    mixed = merge(jnp.concatenate([ya_p, ya_s], axis=0), jnp.concatenate([yb_p, yb_s], axis=0), z,
                  p["w_branch_a"], p["w_branch_b"], TM_BIG, 256)
    t = matmul(mixed, p["w_out"], TM_BIG, 512, "out_proj")
    x1, h2 = post_mix(xp, xs, t, p["g_post_mix"], p["g_pre_ffn"])
    act, conv_p, conv_s = ffn_up(h2, p["w_up"], p["conv_w"], p["conv_b"], state_ffn_conv[l], TM_BIG, FFN_SUB)
    t2 = matmul(act, p["w_down"], TM_DOWN, 256, "down_proj")
    yp, ys = post_ffn(x1, t2, p["g_post_ffn"])

    hs = (N_HEADS, HEAD_WIDTH)
    return (yp.reshape(1, SEQ, D_MODEL),
            ys.reshape(DEC_BATCH, 1, D_MODEL),
            z[:SEQ, COL_K:COL_V].reshape(1, 1, SEQ, *hs),
            z[:SEQ, COL_V:COL_GA].reshape(1, 1, SEQ, *hs),
            zs[:, COL_K:COL_V].reshape(1, DEC_BATCH, 1, *hs),
            zs[:, COL_V:COL_GA].reshape(1, DEC_BATCH, 1, *hs),
            vn_s.reshape(1, DEC_BATCH, 1, A_WIDTH),
            conv_p.reshape(1, 1, CONV_W - 1, 2 * D_FF),
            conv_s.reshape(1, DEC_BATCH, CONV_W - 1, 2 * D_FF))
```

```python
import functools
import math

import numpy as np
import jax
import jax.numpy as jnp
from jax import lax
from jax.experimental import pallas as pl
from jax.experimental.pallas import tpu as pltpu

D_MODEL = 4096
SEQ = 8192
DEC_BATCH = 128
PAST_LEN = 2048
PAGE_SIZE = 128
N_PAGES = PAST_LEN // PAGE_SIZE
A_WIDTH = D_MODEL // 2
A_GROUP_WIDTH = 128
A_GROUPS = A_WIDTH // A_GROUP_WIDTH
CHUNK = 128
HEAD_DIM = 128
HEAD_WIDTH = 2 * HEAD_DIM
N_HEADS = D_MODEL // (4 * HEAD_DIM)
B_WIDTH = N_HEADS * HEAD_WIDTH
D_FF = ((8 * D_MODEL // 3 + 255) // 256) * 256
CONV_W = 3
N_IN = 2 * A_WIDTH + 3 * B_WIDTH + 2 * D_MODEL
EPS = 1e-6
NEG_INF = -1e30
LAM_INIT = 0.8 - 0.6 * math.exp(-0.3 * 0)
Q_SCALE = HEAD_DIM ** -0.5
LOG2E = math.log2(math.e)

COL_U = 0
COL_VA = A_WIDTH
COL_Q = 2 * A_WIDTH
COL_K = COL_Q + B_WIDTH
COL_V = COL_K + B_WIDTH
COL_GA = COL_V + B_WIDTH
COL_GB = COL_GA + D_MODEL

V7X_VMEM_LIMIT_BYTES = 58 * 1024 * 1024
FFN_TN = 256
FFN_PANELS = D_FF // FFN_TN
ATT_T = 512
PAGES_PER_STEP = 8

BF16 = jnp.bfloat16
F32 = jnp.float32


def _params(*sem):
    return pltpu.CompilerParams(dimension_semantics=sem, vmem_limit_bytes=V7X_VMEM_LIMIT_BYTES)


def _gelu(x):
    return 0.5 * x * (1.0 + jnp.tanh(0.7978845608028654 * (x + 0.044715 * (x * x * x))))


def _rms(x, g):
    return x * lax.rsqrt(jnp.mean(x * x, axis=-1, keepdims=True) + EPS) * g


M_ALL = SEQ + DEC_BATCH
ROW_T = DEC_BATCH
N_PROMPT_TILES = SEQ // ROW_T

_ALL_ROWS = pl.BlockSpec((ROW_T, D_MODEL), lambda i: (i, 0))
_PROMPT_ROWS = pl.BlockSpec((ROW_T, D_MODEL), lambda i: (jnp.minimum(i, N_PROMPT_TILES - 1), 0))
_SAMPLE_ROWS = pl.BlockSpec((ROW_T, D_MODEL), lambda i: (0, 0))
_VEC = pl.BlockSpec((1, D_MODEL), lambda i: (0, 0))


def _group_rows(xp_ref, xs_ref):
    return jnp.where(pl.program_id(0) < N_PROMPT_TILES, xp_ref[...], xs_ref[...])


def _rms_cast_kernel(xp_ref, xs_ref, g_ref, o_ref):
    o_ref[...] = _rms(_group_rows(xp_ref, xs_ref), g_ref[...]).astype(o_ref.dtype)


def rms_cast(xp, xs, g):
    return pl.pallas_call(
        _rms_cast_kernel, grid=(M_ALL // ROW_T,),
        in_specs=[_PROMPT_ROWS, _SAMPLE_ROWS, _VEC], out_specs=_ALL_ROWS,
        out_shape=jax.ShapeDtypeStruct((M_ALL, D_MODEL), BF16),
        compiler_params=_params("arbitrary"), name="rms_cast")(xp, xs, g.reshape(1, D_MODEL))


def _post_mix_kernel(xp_ref, xs_ref, t_ref, g1_ref, g2_ref, x1_ref, h_ref):
    x1 = _group_rows(xp_ref, xs_ref) + _rms(t_ref[...], g1_ref[...])
    x1_ref[...] = x1
    h_ref[...] = _rms(x1, g2_ref[...]).astype(h_ref.dtype)


def post_mix(xp, xs, t, g_post, g_pre_next):
    return pl.pallas_call(
        _post_mix_kernel, grid=(M_ALL // ROW_T,),
        in_specs=[_PROMPT_ROWS, _SAMPLE_ROWS, _ALL_ROWS, _VEC, _VEC], out_specs=[_ALL_ROWS, _ALL_ROWS],
        out_shape=[jax.ShapeDtypeStruct((M_ALL, D_MODEL), F32), jax.ShapeDtypeStruct((M_ALL, D_MODEL), BF16)],
        compiler_params=_params("arbitrary"), name="post_mix")(
            xp, xs, t, g_post.reshape(1, D_MODEL), g_pre_next.reshape(1, D_MODEL))


def _post_ffn_kernel(x_ref, t_ref, g_ref, yp_ref, ys_ref):
    y = x_ref[...] + _rms(t_ref[...], g_ref[...])
    i = pl.program_id(0)

    @pl.when(i < N_PROMPT_TILES)
    def _():
        yp_ref[...] = y

    @pl.when(i == N_PROMPT_TILES)
    def _():
        ys_ref[...] = y


def post_ffn(x, t, g):
    return pl.pallas_call(
        _post_ffn_kernel, grid=(M_ALL // ROW_T,),
        in_specs=[_ALL_ROWS, _ALL_ROWS, _VEC], out_specs=[_PROMPT_ROWS, _SAMPLE_ROWS],
        out_shape=[jax.ShapeDtypeStruct((SEQ, D_MODEL), F32), jax.ShapeDtypeStruct((DEC_BATCH, D_MODEL), F32)],
        compiler_params=_params("arbitrary"), name="post_ffn")(x, t, g.reshape(1, D_MODEL))


def _mm_kernel(a_ref, w_ref, o_ref):
    o_ref[...] = jnp.dot(a_ref[...], w_ref[...].astype(BF16),
                         preferred_element_type=F32).astype(o_ref.dtype)


def matmul(a, w, tm, tn, name):
    m, k = a.shape
    n = w.shape[1]
    return pl.pallas_call(
        _mm_kernel, grid=(m // tm, n // tn),
        in_specs=[pl.BlockSpec((tm, k), lambda i, j: (i, 0), pipeline_mode=pl.Buffered(1)),
                  pl.BlockSpec((k, tn), lambda i, j: (0, j))],
        out_specs=pl.BlockSpec((tm, tn), lambda i, j: (i, j)),
        out_shape=jax.ShapeDtypeStruct((m, n), F32),
        compiler_params=_params("arbitrary", "arbitrary"), name=name)(a, w)


def _layernorm(x, g, b):
    mu = jnp.mean(x, axis=-1, keepdims=True)
    xc = x - mu
    var = jnp.mean(xc * xc, axis=-1, keepdims=True)
    return xc * lax.rsqrt(var + EPS) * g + b


def _branch_a_prompt_kernel(u_ref, va_ref, lng_ref, lnb_ref, ws_ref, bst_ref, ya_ref):
    u = _gelu(u_ref[...])
    vn = _layernorm(_gelu(va_ref[...]), lng_ref[...], lnb_ref[...])
    row = lax.broadcasted_iota(jnp.int32, (CHUNK, CHUNK), 0)
    col = lax.broadcasted_iota(jnp.int32, (CHUNK, CHUNK), 1)
    causal = row >= col
    for g in range(A_GROUPS):
        sl = slice(g * A_GROUP_WIDTH, (g + 1) * A_GROUP_WIDTH)
        w = jnp.where(causal, ws_ref[g], 0.0).astype(BF16)
        mixed = jnp.dot(w, vn[:, sl].astype(BF16), preferred_element_type=F32) + bst_ref[:, g:g + 1]
        ya_ref[:, sl] = (u[:, sl] * mixed).astype(ya_ref.dtype)


def branch_a_prompt(z, m, ln_g, ln_b, w_s, b_s):
    vec = pl.BlockSpec((1, A_WIDTH), lambda i: (0, 0))
    return pl.pallas_call(
        _branch_a_prompt_kernel, grid=(m // CHUNK,),
        in_specs=[pl.BlockSpec((CHUNK, A_WIDTH), lambda i: (i, COL_U // A_WIDTH)),
                  pl.BlockSpec((CHUNK, A_WIDTH), lambda i: (i, COL_VA // A_WIDTH)),
                  vec, vec,
                  pl.BlockSpec((A_GROUPS, CHUNK, CHUNK), lambda i: (0, 0, 0)),
                  pl.BlockSpec((CHUNK, A_GROUPS), lambda i: (0, 0))],
        out_specs=pl.BlockSpec((CHUNK, A_WIDTH), lambda i: (i, 0)),
        out_shape=jax.ShapeDtypeStruct((m, A_WIDTH), BF16),
        compiler_params=_params("arbitrary"), name="branch_a_prompt")(
            z, z, ln_g.reshape(1, A_WIDTH), ln_b.reshape(1, A_WIDTH), w_s, b_s.T)


def _branch_a_sample_kernel(u_ref, va_ref, lng_ref, lnb_ref, w00_ref, b0_ref, ya_ref, vn_ref):
    u = _gelu(u_ref[...])
    vn = _layernorm(_gelu(va_ref[...]), lng_ref[...], lnb_ref[...])
    vn_ref[...] = vn
    ya_ref[...] = (u * (vn * w00_ref[...] + b0_ref[...])).astype(ya_ref.dtype)


def branch_a_sample(z, row0, m, ln_g, ln_b, w_s, b_s):
    assert row0 % m == 0
    w00 = jnp.repeat(w_s[:, 0, 0], A_GROUP_WIDTH).reshape(1, A_WIDTH)
    b0 = jnp.repeat(b_s[:, 0], A_GROUP_WIDTH).reshape(1, A_WIDTH)
    vec = pl.BlockSpec((1, A_WIDTH), lambda i: (0, 0))
    out = pl.BlockSpec((m, A_WIDTH), lambda i: (0, 0))
    return pl.pallas_call(
        _branch_a_sample_kernel, grid=(1,),
        in_specs=[pl.BlockSpec((m, A_WIDTH), lambda i: (row0 // m, COL_U // A_WIDTH)),
                  pl.BlockSpec((m, A_WIDTH), lambda i: (row0 // m, COL_VA // A_WIDTH)),
                  vec, vec, vec, vec],
        out_specs=[out, out],
        out_shape=[jax.ShapeDtypeStruct((m, A_WIDTH), BF16), jax.ShapeDtypeStruct((m, A_WIDTH), F32)],
        compiler_params=_params("arbitrary"), name="branch_a_sample")(
            z, z, ln_g.reshape(1, A_WIDTH), ln_b.reshape(1, A_WIDTH), w00, b0)


def _lambda(lq1_ref, lk1_ref, lq2_ref, lk2_ref):
    return (jnp.exp(jnp.sum(lq1_ref[...] * lk1_ref[...], axis=1, keepdims=True))
            - jnp.exp(jnp.sum(lq2_ref[...] * lk2_ref[...], axis=1, keepdims=True)) + LAM_INIT)


def _alibi_slopes():
    return np.asarray(2.0 ** (-8.0 * np.arange(1, N_HEADS + 1, dtype=np.float32) / N_HEADS), dtype=np.float32)


N_C_PIECES = 3
EXTRA_COLS = 4 * N_C_PIECES


def _alibi_pieces():
    out = []
    for s in _alibi_slopes().astype(np.float64):
        rest = s * math.log2(math.e)
        for _ in range(N_C_PIECES):
            piece = float(np.asarray(rest, np.float32).astype(BF16).astype(np.float32))
            out.append(piece)
            rest -= piece
    return np.asarray(out, np.float32)


def _split_pos(pos):
    lo = pos % 256
    return lo.astype(F32), (pos - lo).astype(F32)


def _qkv_prep_kernel(c_ref, q_ref, k_ref, v_ref, qt_ref, ka_ref, vt_ref):
    t_tile = ATT_T
    h = pl.program_id(1)
    cs = [c_ref[h * N_C_PIECES + t] for t in range(N_C_PIECES)]

    qt = (q_ref[...] * (Q_SCALE * LOG2E)).T
    row = lax.broadcasted_iota(jnp.int32, (HEAD_DIM, t_tile), 0)
    i_lo, i_hi = _split_pos(lax.broadcasted_iota(jnp.int32, (HEAD_DIM, t_tile), 1))
    qe = jnp.zeros((HEAD_DIM, t_tile), F32)
    for t in range(N_C_PIECES):
        qe = jnp.where(row == 4 * t, i_lo, qe)
        qe = jnp.where(row == 4 * t + 1, i_hi, qe)
        qe = jnp.where((row == 4 * t + 2) | (row == 4 * t + 3), cs[t], qe)
    qt_ref[0] = jnp.concatenate([qt[:HEAD_DIM], qe], axis=0).astype(BF16)
    qt_ref[1] = jnp.concatenate([qt[HEAD_DIM:], qe], axis=0).astype(BF16)

    k = k_ref[...]
    col = lax.broadcasted_iota(jnp.int32, (t_tile, HEAD_DIM), 1)
    j_lo, j_hi = _split_pos(lax.broadcasted_iota(jnp.int32, (t_tile, HEAD_DIM), 0))
    ke = jnp.zeros((t_tile, HEAD_DIM), F32)
    for t in range(N_C_PIECES):
        ke = jnp.where((col == 4 * t) | (col == 4 * t + 1), -cs[t], ke)
        ke = jnp.where(col == 4 * t + 2, j_lo, ke)
        ke = jnp.where(col == 4 * t + 3, j_hi, ke)
    ka_ref[0] = jnp.concatenate([k[:, :HEAD_DIM], ke], axis=1).astype(BF16)
    ka_ref[1] = jnp.concatenate([k[:, HEAD_DIM:], ke], axis=1).astype(BF16)

    vt_ref[...] = v_ref[...].T.astype(BF16)


def qkv_prep(z, seq):
    t_tile = ATT_T
    nt = seq // t_tile
    qb, kb, vb = COL_Q // HEAD_WIDTH, COL_K // HEAD_WIDTH, COL_V // HEAD_WIDTH
    grid_spec = pltpu.PrefetchScalarGridSpec(
        num_scalar_prefetch=1, grid=(nt, N_HEADS),
        in_specs=[pl.BlockSpec((t_tile, HEAD_WIDTH), lambda i, h, c: (i, qb + h)),
                  pl.BlockSpec((t_tile, HEAD_WIDTH), lambda i, h, c: (i, kb + h)),
                  pl.BlockSpec((t_tile, HEAD_WIDTH), lambda i, h, c: (i, vb + h))],
        out_specs=[pl.BlockSpec((None, 2, HEAD_WIDTH, t_tile), lambda i, h, c: (h, 0, 0, i)),
                   pl.BlockSpec((None, 2, None, t_tile, HEAD_WIDTH), lambda i, h, c: (h, 0, i, 0, 0)),
                   pl.BlockSpec((None, None, HEAD_WIDTH, t_tile), lambda i, h, c: (h, i, 0, 0))])
    return pl.pallas_call(
        _qkv_prep_kernel, grid_spec=grid_spec,
        out_shape=[jax.ShapeDtypeStruct((N_HEADS, 2, HEAD_WIDTH, seq), BF16),
                   jax.ShapeDtypeStruct((N_HEADS, 2, nt, t_tile, HEAD_WIDTH), BF16),
                   jax.ShapeDtypeStruct((N_HEADS, nt, HEAD_WIDTH, t_tile), BF16)],
        compiler_params=_params("arbitrary", "arbitrary"), name="qkv_prep")(
            jnp.asarray(_alibi_pieces()), z, z, z)


HEAD_GROUPS = 2


def _attn_pending_chain(n, kv, vt_refs, p_ref, al_ref, acc_ref):
    vt = vt_refs[n // 2][kv]
    acc_ref[n] = acc_ref[n] * al_ref[n] + jnp.dot(vt, p_ref[n], preferred_element_type=F32)


def _attn_scores_chain(n, kj, coff, masked, qt_refs, ka_refs, p_ref, al_ref, m_prev, l_prev):
    t_tile = ATT_T
    g, half = n // 2, n % 2
    s = jnp.dot(ka_refs[g][half, kj], qt_refs[g][half], preferred_element_type=F32)
    if masked:
        key = lax.broadcasted_iota(jnp.int32, (t_tile, t_tile), 0)
        qry = lax.broadcasted_iota(jnp.int32, (t_tile, t_tile), 1)
        s = jnp.where(key > qry, NEG_INF, s)
    m_next = jnp.maximum(m_prev, jnp.max(s, axis=0, keepdims=True) + coff)
    alpha = jnp.exp2(m_prev - m_next)
    p = jnp.exp2(s - (m_next - coff))
    p_ref[n] = p.astype(BF16)
    al_ref[n] = alpha
    return m_next, alpha * l_prev + jnp.sum(p, axis=0, keepdims=True)


SIDE_UNITS = 3
SIDE_GROUPS = 3
SIDE_RING = SIDE_UNITS * SIDE_GROUPS


def _side_copies(unit, pos, pt_ref, ck_ref, cv_ref, kbuf, vbuf, sems):
    page = pt_ref[unit]
    return (pltpu.make_async_copy(ck_ref.at[page], kbuf.at[pos], sems.at[0, pos]),
            pltpu.make_async_copy(cv_ref.at[page], vbuf.at[pos], sems.at[1, pos]))


def _side_scores(unit, pos, state, side):
    (sq_ref, sk_ref, sv_ref, slope_ref, sgr_ref, kbuf, vbuf, m_st, l_st, acc_st, so_ref) = side
    rows = 2 * N_HEADS
    b = unit // N_PAGES
    page_no = unit % N_PAGES
    first = page_no == 0

    q8 = sq_ref[b] * Q_SCALE
    lane = lax.broadcasted_iota(jnp.int32, (N_HEADS, HEAD_WIDTH), 1)
    qmat = jnp.concatenate([jnp.where(lane < HEAD_DIM, q8, 0.0),
                            jnp.where(lane >= HEAD_DIM, q8, 0.0)], axis=0)
    kp = kbuf[pos].astype(BF16)
    s = lax.dot_general(qmat.astype(BF16), kp, (((1,), (1,)), ((), ())), preferred_element_type=F32)
    width = PAGE_SIZE * N_HEADS
    col = lax.broadcasted_iota(jnp.int32, (rows, width), 1)
    row = lax.broadcasted_iota(jnp.int32, (rows, width), 0)
    kpos = page_no * PAGE_SIZE + col // N_HEADS
    bias = -slope_ref[:, :1] * (PAST_LEN - kpos).astype(F32)
    s = jnp.where(col % N_HEADS == row % N_HEADS, s + bias, NEG_INF)

    kn = sk_ref[b]
    vn = sv_ref[b]
    s_new = jnp.sum(qmat * jnp.concatenate([kn, kn], axis=0), axis=1, keepdims=True)
    m_prev = jnp.where(first, jnp.broadcast_to(s_new, m_st.shape), state[0])
    l_prev = jnp.where(first, 1.0, state[1])
    acc_prev = jnp.where(first, jnp.concatenate([vn, vn], axis=0), state[2])

    m_next = jnp.maximum(m_prev, jnp.max(s, axis=1, keepdims=True))
    alpha = jnp.exp(m_prev - m_next)
    p = jnp.exp(s - m_next[:, :1])
    l_next = alpha * l_prev + jnp.sum(p, axis=1, keepdims=True)
    return p.astype(BF16), m_next, l_next, acc_prev * alpha[:, :1]


def _side_values(unit, pos, scored, lam, side):
    (sq_ref, sk_ref, sv_ref, slope_ref, sgr_ref, kbuf, vbuf, m_st, l_st, acc_st, so_ref) = side
    p, m_next, l_next, acc_scaled = scored
    acc = acc_scaled + jnp.dot(p, vbuf[pos].astype(BF16), preferred_element_type=F32)
    o = acc / l_next[:, :1]
    out = o[:N_HEADS] - lam * o[N_HEADS:]
    so_ref[unit // N_PAGES] = _rms(out, sgr_ref[...]) * (1.0 - LAM_INIT)
    return m_next, l_next, acc


def _attn_slot(g, n_slots, kj, kv_pending, coffs, masked, stats, lam, main, dma, side):
    qt_refs, ka_refs, vt_refs, acc_ref, p_ref, al_ref = main
    pt_ref, ck_ref, cv_ref, sems = dma
    kbuf, vbuf, m_st, l_st, acc_st = side[5], side[6], side[7], side[8], side[9]

    def start_slot(slot):
        for k in range(SIDE_UNITS):
            pos = (slot % SIDE_GROUPS) * SIDE_UNITS + k
            for cp in _side_copies(slot * SIDE_UNITS + k, pos, pt_ref, ck_ref, cv_ref, kbuf, vbuf, sems):
                cp.start()

    @pl.when(g == 0)
    def _():
        for slot in range(SIDE_GROUPS - 1):
            start_slot(slot)

    @pl.when(g + SIDE_GROUPS - 1 < n_slots)
    def _():
        start_slot(g + SIDE_GROUPS - 1)

    units = [(g * SIDE_UNITS + k, (g % SIDE_GROUPS) * SIDE_UNITS + k) for k in range(SIDE_UNITS)]
    for unit, pos in units:
        for cp in _side_copies(unit, pos, pt_ref, ck_ref, cv_ref, kbuf, vbuf, sems):
            cp.wait()

    n_chain = 2 * HEAD_GROUPS
    pending = [functools.partial(_attn_pending_chain, n, kv_pending, vt_refs, p_ref, al_ref, acc_ref)
               for n in range(n_chain)]
    new_stats = list(stats)

    def scores(n):
        new_stats[2 * n], new_stats[2 * n + 1] = _attn_scores_chain(
            n, kj, coffs[n // 2], masked, qt_refs, ka_refs, p_ref, al_ref, stats[2 * n], stats[2 * n + 1])

    state = (m_st[...], l_st[...], acc_st[...])
    scored = _side_scores(*units[0], state, side)
    pending[0]()
    scores(0)
    state = _side_values(*units[0], scored, lam, side)
    scored = _side_scores(*units[1], state, side)
    pending[1]()
    scores(1)
    state = _side_values(*units[1], scored, lam, side)
    scored = _side_scores(*units[2], state, side)
    pending[2]()
    scores(2)
    pending[3]()
    scores(3)
    state = _side_values(*units[2], scored, lam, side)
    m_st[...], l_st[...], acc_st[...] = state
    return tuple(new_stats)


def _attn_prompt_kernel(c_ref, pt_ref,
                        qt_a, qt_b, ka_a, ka_b, vt_a, vt_b, lq1_ref, lk1_ref, lq2_ref, lk2_ref, sg_ref,
                        sgr_ref, sq_ref, sk_ref, sv_ref, slope_ref, ck_ref, cv_ref,
                        o_a, o_b, so_ref,
                        acc_ref, p_ref, al_ref, kbuf, vbuf, sems, m_st, l_st, acc_st):
    assert SIDE_UNITS == 3 and HEAD_GROUPS == 2, "_attn_slot writes its interleaving out for these counts"
    t_tile = ATT_T
    o_refs = (o_a, o_b)
    main = ((qt_a, qt_b), (ka_a, ka_b), (vt_a, vt_b), acc_ref, p_ref, al_ref)
    dma = (pt_ref, ck_ref, cv_ref, sems)
    side = (sq_ref, sk_ref, sv_ref, slope_ref, sgr_ref, kbuf, vbuf, m_st, l_st, acc_st, so_ref)
    nt = ka_a.shape[1]
    slots_per_pair = nt * (nt + 1) // 2
    n_slots = pl.num_programs(0) * slots_per_pair
    hp = pl.program_id(0)
    qi = pl.program_id(1)
    g_base = hp * slots_per_pair + (qi * (qi + 1)) // 2
    c_sums = []
    for g in range(HEAD_GROUPS):
        base = (hp + g * (N_HEADS // HEAD_GROUPS)) * N_C_PIECES
        c_sum = c_ref[base]
        for t in range(1, N_C_PIECES):
            c_sum = c_sum + c_ref[base + t]
        c_sums.append(c_sum)
    lam = _lambda(lq1_ref, lk1_ref, lq2_ref, lk2_ref)

    @pl.when((hp == 0) & (qi == 0))
    def _():
        so_ref[...] = jnp.zeros(so_ref.shape, F32)
        m_st[...] = jnp.zeros(m_st.shape, F32)
        l_st[...] = jnp.zeros(l_st.shape, F32)
        acc_st[...] = jnp.zeros(acc_st.shape, F32)

    acc_ref[...] = jnp.zeros(acc_ref.shape, F32)
    p_ref[...] = jnp.zeros(p_ref.shape, BF16)
    al_ref[...] = jnp.ones(al_ref.shape, F32)
    neg = jnp.full((1, t_tile), NEG_INF, F32)
    zero = jnp.zeros((1, t_tile), F32)

    def body(kj, stats):
        dist = ((qi - kj) * t_tile).astype(F32)
        return _attn_slot(g_base + kj, n_slots, kj, jnp.maximum(kj - 1, 0), [-c * dist for c in c_sums], False,
                          stats, lam, main, dma, side)

    stats = lax.fori_loop(0, qi, body, (neg, zero) * (2 * HEAD_GROUPS))
    stats = _attn_slot(g_base + qi, n_slots, qi, jnp.maximum(qi - 1, 0), [0.0] * HEAD_GROUPS, True,
                       stats, lam, main, dma, side)
    for n in range(2 * HEAD_GROUPS):
        _attn_pending_chain(n, qi, main[2], p_ref, al_ref, acc_ref)

    for g in range(HEAD_GROUPS):
        l1, l2 = stats[4 * g + 1], stats[4 * g + 3]
        out = acc_ref[2 * g] / l1 - lam * (acc_ref[2 * g + 1] / l2)
        ms = jnp.mean(out * out, axis=0, keepdims=True)
        y = out * lax.rsqrt(ms + EPS) * sg_ref[...] * (1.0 - LAM_INIT)
        o_refs[g][...] = y.T.astype(o_refs[g].dtype)


def _sample_heads(zs, c0):
    return zs[:, c0:c0 + B_WIDTH].reshape(zs.shape[0], N_HEADS, HEAD_WIDTH)


def _slope_rows():
    return jnp.asarray(np.broadcast_to(np.tile(_alibi_slopes(), 2)[:, None], (2 * N_HEADS, 128)).copy())


def side_sequences(seq):
    nt = seq // ATT_T
    n_units = SIDE_UNITS * (N_HEADS // HEAD_GROUPS) * (nt * (nt + 1) // 2)
    assert n_units % N_PAGES == 0, "the side stream must end on a sequence boundary"
    return n_units // N_PAGES


def attn_prompt(z, seq, zs, cache_k, cache_v, page_table, lq1, lk1, lq2, lk2, subln_g):
    t_tile = ATT_T
    nt = seq // t_tile
    hg = N_HEADS // HEAD_GROUPS
    nb = zs.shape[0]
    assert side_sequences(seq) <= nb
    n_pool = cache_k.shape[0]
    page_rows = PAGE_SIZE * N_HEADS
    ck = cache_k.reshape(n_pool, page_rows, HEAD_WIDTH)
    cv = cache_v.reshape(n_pool, page_rows, HEAD_WIDTH)
    qt, ka, vt = qkv_prep(z, seq)
    const = lambda shape: pl.BlockSpec(shape, lambda h, i, c, pt: (0,) * len(shape))
    whole = lambda shape: pl.BlockSpec(shape, lambda h, i, c, pt: (0,) * len(shape), pipeline_mode=pl.Buffered(1))
    lvec = const((1, HEAD_DIM))
    qspec = lambda g: pl.BlockSpec((None, 2, HEAD_WIDTH, t_tile), lambda h, i, c, pt: (h + g * hg, 0, 0, i))
    kspec = lambda g: pl.BlockSpec((None, 2, nt, t_tile, HEAD_WIDTH), lambda h, i, c, pt: (h + g * hg, 0, 0, 0, 0),
                                   pipeline_mode=pl.Buffered(1))
    vspec = lambda g: pl.BlockSpec((None, nt, HEAD_WIDTH, t_tile), lambda h, i, c, pt: (h + g * hg, 0, 0, 0),
                                   pipeline_mode=pl.Buffered(1))
    ospec = pl.BlockSpec((t_tile, HEAD_WIDTH), lambda h, i, c, pt: (i, h))
    heads = (nb, N_HEADS, HEAD_WIDTH)
    n_chain = 2 * HEAD_GROUPS
    grid_spec = pltpu.PrefetchScalarGridSpec(
        num_scalar_prefetch=2, grid=(hg, nt),
        in_specs=[qspec(0), qspec(1), kspec(0), kspec(1), vspec(0), vspec(1),
                  lvec, lvec, lvec, lvec, const((HEAD_WIDTH, 1)), const((1, HEAD_WIDTH)),
                  whole(heads), whole(heads), whole(heads), const((2 * N_HEADS, 128)),
                  pl.BlockSpec(memory_space=pl.ANY), pl.BlockSpec(memory_space=pl.ANY)],
        out_specs=[ospec, ospec, const(heads)],
        scratch_shapes=[pltpu.VMEM((n_chain, HEAD_WIDTH, t_tile), F32),
                        pltpu.VMEM((n_chain, t_tile, t_tile), BF16),
                        pltpu.VMEM((n_chain, 1, t_tile), F32),
                        pltpu.VMEM((SIDE_RING, page_rows, HEAD_WIDTH), F32),
                        pltpu.VMEM((SIDE_RING, page_rows, HEAD_WIDTH), F32),
                        pltpu.SemaphoreType.DMA((2, SIDE_RING)),
                        pltpu.VMEM((2 * N_HEADS, 128), F32),
                        pltpu.VMEM((2 * N_HEADS, 128), F32),
                        pltpu.VMEM((2 * N_HEADS, HEAD_WIDTH), F32)])
    r = lambda a: a.reshape(1, -1)
    half_out = jax.ShapeDtypeStruct((seq, B_WIDTH // HEAD_GROUPS), BF16)
    y_a, y_b, y_s = pl.pallas_call(
        _attn_prompt_kernel, grid_spec=grid_spec,
        out_shape=[half_out, half_out, jax.ShapeDtypeStruct(heads, F32)],
        compiler_params=_params("arbitrary", "arbitrary"), name="attn_prompt")(
            jnp.asarray(_alibi_pieces()), page_table.reshape(-1),
            qt, qt, ka, ka, vt, vt, r(lq1), r(lk1), r(lq2), r(lk2),
            subln_g.reshape(HEAD_WIDTH, 1), r(subln_g),
            _sample_heads(zs, COL_Q), _sample_heads(zs, COL_K), _sample_heads(zs, COL_V), _slope_rows(), ck, cv)
    return jnp.concatenate([y_a, y_b], axis=1), y_s


def _attn_sample_kernel(pt_ref, q_ref, kn_ref, vn_ref, slope_ref, lq1_ref, lk1_ref, lq2_ref, lk2_ref, sg_ref,
                        *rest):
    npg = PAGES_PER_STEP
    k_refs = rest[:npg]
    v_refs = rest[npg:2 * npg]
    o_ref, qmat_ref, m_ref, l_ref, acc_ref = rest[2 * npg:]
    s_id = pl.program_id(1)
    rows = 2 * N_HEADS

    @pl.when(s_id == 0)
    def _():
        q8 = q_ref[...] * Q_SCALE
        lane = lax.broadcasted_iota(jnp.int32, (N_HEADS, HEAD_WIDTH), 1)
        qmat = jnp.concatenate([jnp.where(lane < HEAD_DIM, q8, 0.0),
                                jnp.where(lane >= HEAD_DIM, q8, 0.0)], axis=0)
        qmat_ref[...] = qmat.astype(BF16)
        kn = kn_ref[...]
        vn = vn_ref[...]
        s_new = jnp.sum(qmat * jnp.concatenate([kn, kn], axis=0), axis=1, keepdims=True)
        m_ref[...] = jnp.broadcast_to(s_new, m_ref.shape)
        l_ref[...] = jnp.ones(l_ref.shape, F32)
        acc_ref[...] = jnp.concatenate([vn, vn], axis=0)

    qmat = qmat_ref[...]
    scores = []
    for i in range(npg):
        kp = k_refs[i][...].astype(BF16)
        scores.append(lax.dot_general(qmat, kp, (((1,), (1,)), ((), ())), preferred_element_type=F32))
    width = npg * PAGE_SIZE * N_HEADS
    col = lax.broadcasted_iota(jnp.int32, (rows, width), 1)
    row = lax.broadcasted_iota(jnp.int32, (rows, width), 0)
    pos = s_id * (npg * PAGE_SIZE) + col // N_HEADS
    bias = -slope_ref[:, :1] * (PAST_LEN - pos).astype(F32)
    s = jnp.where(col % N_HEADS == row % N_HEADS, jnp.concatenate(scores, axis=1) + bias, NEG_INF)
    m_prev = m_ref[...]
    m_next = jnp.maximum(m_prev, jnp.max(s, axis=1, keepdims=True))
    alpha = jnp.exp(m_prev - m_next)
    p = jnp.exp(s - m_next[:, :1])
    l_ref[...] = alpha * l_ref[...] + jnp.sum(p, axis=1, keepdims=True)
    m_ref[...] = m_next
    acc = acc_ref[...] * alpha[:, :1]
    pw = PAGE_SIZE * N_HEADS
    for i in range(npg):
        acc = acc + jnp.dot(p[:, i * pw:(i + 1) * pw].astype(BF16),
                            v_refs[i][...].astype(BF16), preferred_element_type=F32)
    acc_ref[...] = acc

    @pl.when(s_id == pl.num_programs(1) - 1)
    def _():
        lam = _lambda(lq1_ref, lk1_ref, lq2_ref, lk2_ref)
        o = acc_ref[...] / l_ref[...][:, :1]
        out = o[:N_HEADS] - lam * o[N_HEADS:]
        o_ref[...] = _rms(out, sg_ref[...]) * (1.0 - LAM_INIT)


def attn_sample(z, b0, cache_k, cache_v, page_table, lq1, lk1, lq2, lk2, subln_g):
    nb = z.shape[0] - b0
    npg = PAGES_PER_STEP
    n_steps = N_PAGES // npg
    n_pool = cache_k.shape[0]
    ck = cache_k.reshape(n_pool, PAGE_SIZE * N_HEADS, HEAD_WIDTH)
    cv = cache_v.reshape(n_pool, PAGE_SIZE * N_HEADS, HEAD_WIDTH)
    inblk = pl.BlockSpec((None, N_HEADS, HEAD_WIDTH), lambda b, s, pt: (b0 + b, 0, 0))
    outblk = pl.BlockSpec((None, N_HEADS, HEAD_WIDTH), lambda b, s, pt: (b, 0, 0))
    lvec = pl.BlockSpec((1, HEAD_DIM), lambda b, s, pt: (0, 0))

    def page_spec(i):
        return pl.BlockSpec((None, PAGE_SIZE * N_HEADS, HEAD_WIDTH),
                            lambda b, s, pt: (pt[(b0 + b) * N_PAGES + s * npg + i], 0, 0))

    grid_spec = pltpu.PrefetchScalarGridSpec(
        num_scalar_prefetch=1, grid=(nb, n_steps),
        in_specs=[inblk, inblk, inblk,
                  pl.BlockSpec((2 * N_HEADS, 128), lambda b, s, pt: (0, 0)),
                  lvec, lvec, lvec, lvec,
                  pl.BlockSpec((1, HEAD_WIDTH), lambda b, s, pt: (0, 0))]
                 + [page_spec(i) for i in range(npg)] + [page_spec(i) for i in range(npg)],
        out_specs=outblk,
        scratch_shapes=[pltpu.VMEM((2 * N_HEADS, HEAD_WIDTH), BF16),
                        pltpu.VMEM((2 * N_HEADS, 128), F32),
                        pltpu.VMEM((2 * N_HEADS, 128), F32),
                        pltpu.VMEM((2 * N_HEADS, HEAD_WIDTH), F32)])
    r = lambda a: a.reshape(1, -1)
    return pl.pallas_call(
        _attn_sample_kernel, grid_spec=grid_spec,
        out_shape=jax.ShapeDtypeStruct((nb, N_HEADS, HEAD_WIDTH), F32),
        compiler_params=_params("arbitrary", "arbitrary"), name="attn_sample")(
            page_table.reshape(-1), _sample_heads(z, COL_Q), _sample_heads(z, COL_K), _sample_heads(z, COL_V),
            _slope_rows(), r(lq1), r(lk1), r(lq2), r(lk2), r(subln_g), *([ck] * npg), *([cv] * npg))


def _merge_kernel(ya_ref, yb_ref, wa_ref, wb_ref, ga_ref, gb_ref, o_ref):
    a = jnp.dot(ya_ref[...], wa_ref[...].astype(BF16), preferred_element_type=F32)
    b = jnp.dot(yb_ref[...], wb_ref[...].astype(BF16), preferred_element_type=F32)
    o_ref[...] = (jax.nn.sigmoid(ga_ref[...]) * a + jax.nn.sigmoid(gb_ref[...]) * b).astype(o_ref.dtype)


def merge(ya, yb, z, w_a, w_b, tm, tn):
    m = ya.shape[0]
    act = lambda: pl.BlockSpec((tm, A_WIDTH), lambda i, j: (i, 0), pipeline_mode=pl.Buffered(1))
    wsp = pl.BlockSpec((A_WIDTH, tn), lambda i, j: (0, j))
    return pl.pallas_call(
        _merge_kernel, grid=(m // tm, D_MODEL // tn),
        in_specs=[act(), act(), wsp, wsp,
                  pl.BlockSpec((tm, tn), lambda i, j: (i, COL_GA // tn + j)),
                  pl.BlockSpec((tm, tn), lambda i, j: (i, COL_GB // tn + j))],
        out_specs=pl.BlockSpec((tm, tn), lambda i, j: (i, j)),
        out_shape=jax.ShapeDtypeStruct((m, D_MODEL), BF16),
        compiler_params=_params("arbitrary", "arbitrary"), name="merge")(ya, yb, w_a, w_b, z, z)


def _conv3(up, prev1, prev2, cw_ref, cb_ref, own_state=None):
    rows = lax.broadcasted_iota(jnp.int32, up.shape, 0)
    s1 = jnp.where(rows == 0, prev1, pltpu.roll(up, 1, axis=0))
    s2 = jnp.where(rows == 0, prev2, jnp.where(rows == 1, prev1, pltpu.roll(up, 2, axis=0)))
    if own_state is not None:
        mask, st1, st0 = own_state
        s1 = jnp.where(mask, st1, s1)
        s2 = jnp.where(mask, st0, s2)
    return cb_ref[...] + cw_ref[0:1, :] * s2 + cw_ref[1:2, :] * s1 + cw_ref[2:3, :] * up


def _ffn_up_kernel(h_ref, wg_ref, wv_ref, cwg_ref, cwv_ref, cbg_ref, cbv_ref,
                   s0g_ref, s0v_ref, s1g_ref, s1v_ref,
                   act_ref, tailg_ref, tailv_ref, upsg_ref, upsv_ref, carry_ref, *, subs):
    i = pl.program_id(0)
    j = pl.program_id(1)
    n_samp = s0g_ref.shape[0]
    n_sub = len(subs)
    edge = subs[-1] - n_samp
    wg = wg_ref[...].astype(BF16)
    wv = wv_ref[...].astype(BF16)

    @pl.when(i == 0)
    def _():
        carry_ref[j] = jnp.zeros(carry_ref.shape[1:], F32)

    prev = carry_ref[j]
    pg1, pg2 = prev[7:8, :FFN_TN], prev[6:7, :FFN_TN]
    pv1, pv2 = prev[7:8, FFN_TN:], prev[6:7, FFN_TN:]
    for r, sub in enumerate(subs):
        rows = slice(sum(subs[:r]), sum(subs[:r + 1]))
        h = h_ref[rows, :]
        upg = jnp.dot(h, wg, preferred_element_type=F32)
        upv = jnp.dot(h, wv, preferred_element_type=F32)
        own_g = own_v = None
        if r == n_sub - 1:
            local = lax.broadcasted_iota(jnp.int32, (sub, FFN_TN), 0)
            mask = (local >= edge) & (i == pl.num_programs(0) - 1)
            pad = lambda ref: jnp.concatenate([jnp.zeros((edge, FFN_TN), F32), ref[...]], axis=0)
            own_g = (mask, pad(s1g_ref), pad(s0g_ref))
            own_v = (mask, pad(s1v_ref), pad(s0v_ref))
        cg = _conv3(upg, pg1, pg2, cwg_ref, cbg_ref, own_g)
        cv = _conv3(upv, pv1, pv2, cwv_ref, cbv_ref, own_v)
        act_ref[rows, :] = (_gelu(cg) * cv).astype(act_ref.dtype)
        pg1, pg2 = upg[sub - 1:sub, :], upg[sub - 2:sub - 1, :]
        pv1, pv2 = upv[sub - 1:sub, :], upv[sub - 2:sub - 1, :]
    carry_ref[j] = jnp.concatenate([upg[sub - 8:, :], upv[sub - 8:, :]], axis=1)
    tailg_ref[...] = upg[edge - 8:edge, :]
    tailv_ref[...] = upv[edge - 8:edge, :]
    upsg_ref[...] = upg[edge:, :]
    upsv_ref[...] = upv[edge:, :]


def ffn_up(h, w_up, conv_w, conv_b, state, subs):
    m = h.shape[0]
    tm = sum(subs)
    n_mt = m // tm
    n_samp = state.shape[0]
    assert m % tm == 0 and subs[-1] >= n_samp + 8 and all(s % 16 == 0 for s in subs)
    np_ = FFN_PANELS
    cb = conv_b.reshape(1, 2 * D_FF)
    st = state.reshape(n_samp, (CONV_W - 1) * 2 * D_FF)
    sblk = lambda off: pl.BlockSpec((n_samp, FFN_TN), lambda i, j: (0, off + j))
    tail = pl.BlockSpec((None, 8, FFN_TN), lambda i, j: (i, 0, j))
    ups = pl.BlockSpec((None, n_samp, FFN_TN), lambda i, j: (i, 0, j))
    act, tail_g, tail_v, ups_g, ups_v = pl.pallas_call(
        functools.partial(_ffn_up_kernel, subs=tuple(subs)), grid=(n_mt, np_),
        in_specs=[pl.BlockSpec((tm, D_MODEL), lambda i, j: (i, 0), pipeline_mode=pl.Buffered(1)),
                  pl.BlockSpec((D_MODEL, FFN_TN), lambda i, j: (0, j)),
                  pl.BlockSpec((D_MODEL, FFN_TN), lambda i, j: (0, np_ + j)),
                  pl.BlockSpec((CONV_W, FFN_TN), lambda i, j: (0, j)),
                  pl.BlockSpec((CONV_W, FFN_TN), lambda i, j: (0, np_ + j)),
                  pl.BlockSpec((1, FFN_TN), lambda i, j: (0, j)),
                  pl.BlockSpec((1, FFN_TN), lambda i, j: (0, np_ + j)),
                  sblk(0), sblk(np_), sblk(2 * np_), sblk(3 * np_)],
        out_specs=[pl.BlockSpec((tm, FFN_TN), lambda i, j: (i, j)), tail, tail, ups, ups],
        out_shape=[jax.ShapeDtypeStruct((m, D_FF), BF16),
                   jax.ShapeDtypeStruct((n_mt, 8, D_FF), F32),
                   jax.ShapeDtypeStruct((n_mt, 8, D_FF), F32),
                   jax.ShapeDtypeStruct((n_mt, n_samp, D_FF), F32),
                   jax.ShapeDtypeStruct((n_mt, n_samp, D_FF), F32)],
        scratch_shapes=[pltpu.VMEM((np_, 8, 2 * FFN_TN), F32)],
        compiler_params=_params("arbitrary", "arbitrary"), name="ffn_up")(
            h, w_up, w_up, conv_w, conv_w, cb, cb, st, st, st, st)
    keep = 8 - (CONV_W - 1)
    conv_prompt = jnp.concatenate([tail_g[n_mt - 1, keep:], tail_v[n_mt - 1, keep:]], axis=-1)
    up_s = jnp.concatenate([ups_g[n_mt - 1], ups_v[n_mt - 1]], axis=-1)
    conv_sample = jnp.concatenate([state[:, 1:], up_s[:, None, :]], axis=1)
    return act, conv_prompt, conv_sample


TM_BIG = M_ALL // 4
TM_DOWN = M_ALL // 8
FFN_SUBS = (TM_BIG // 5,) * 5
assert sum(FFN_SUBS) == TM_BIG


def kernel(x_prompt, x_sample, cache_k, cache_v, state_ffn_conv, page_table, g_pre_mix, w_in, a_ln_g, a_ln_b, a_w_s, a_b_s, lambda_q1, lambda_k1, lambda_q2, lambda_k2, subln_g, w_branch_a, w_branch_b, w_out, g_post_mix, g_pre_ffn, w_up, conv_w, conv_b, w_down, g_post_ffn):
    l = 0
    p = dict(g_pre_mix=g_pre_mix[l], w_in=w_in[l], a_ln_g=a_ln_g[l], a_ln_b=a_ln_b[l], a_w_s=a_w_s[l],
             a_b_s=a_b_s[l], w_branch_a=w_branch_a[l], w_branch_b=w_branch_b[l], w_out=w_out[l],
             g_post_mix=g_post_mix[l], g_pre_ffn=g_pre_ffn[l], w_up=w_up[l], conv_w=conv_w[l],
             conv_b=conv_b[l], w_down=w_down[l], g_post_ffn=g_post_ffn[l])
    lams = (lambda_q1[l], lambda_k1[l], lambda_q2[l], lambda_k2[l], subln_g[l])
    gate_args = (p["a_ln_g"], p["a_ln_b"], p["a_w_s"], p["a_b_s"])

    xp = x_prompt.reshape(SEQ, D_MODEL)
    xs = x_sample.reshape(DEC_BATCH, D_MODEL)
    h = rms_cast(xp, xs, p["g_pre_mix"])
    z = matmul(h, p["w_in"], TM_BIG, 512, "in_proj")
    zs = z[SEQ:]

    n_side = side_sequences(SEQ)
    yb_p, yb_side = attn_prompt(z, SEQ, zs, cache_k[l], cache_v[l], page_table, *lams)
    yb_rest = attn_sample(zs, n_side, cache_k[l], cache_v[l], page_table, *lams)
    yb_s = jnp.concatenate([yb_side[:n_side], yb_rest], axis=0).reshape(DEC_BATCH, B_WIDTH).astype(BF16)
    ya_p = branch_a_prompt(z, SEQ, *gate_args)
    ya_s, vn_s = branch_a_sample(z, SEQ, DEC_BATCH, *gate_args)

    mixed = merge(jnp.concatenate([ya_p, ya_s], axis=0), jnp.concatenate([yb_p, yb_s], axis=0), z,
                  p["w_branch_a"], p["w_branch_b"], TM_BIG, 256)
    t = matmul(mixed, p["w_out"], TM_BIG, 512, "out_proj")
    x1, h2 = post_mix(xp, xs, t, p["g_post_mix"], p["g_pre_ffn"])
    act, conv_p, conv_s = ffn_up(h2, p["w_up"], p["conv_w"], p["conv_b"], state_ffn_conv[l], FFN_SUBS)
    t2 = matmul(act, p["w_down"], TM_DOWN, 256, "down_proj")
    yp, ys = post_ffn(x1, t2, p["g_post_ffn"])

    hs = (N_HEADS, HEAD_WIDTH)
    return (yp.reshape(1, SEQ, D_MODEL),
            ys.reshape(DEC_BATCH, 1, D_MODEL),
            z[:SEQ, COL_K:COL_V].reshape(1, 1, SEQ, *hs),
            z[:SEQ, COL_V:COL_GA].reshape(1, 1, SEQ, *hs),
            zs[:, COL_K:COL_V].reshape(1, DEC_BATCH, 1, *hs),
            zs[:, COL_V:COL_GA].reshape(1, DEC_BATCH, 1, *hs),
            vn_s.reshape(1, DEC_BATCH, 1, A_WIDTH),
            conv_p.reshape(1, 1, CONV_W - 1, 2 * D_FF),
            conv_s.reshape(1, DEC_BATCH, CONV_W - 1, 2 * D_FF))
```

```python
import functools
import math

import numpy as np
import jax
import jax.numpy as jnp
from jax import lax
from jax.experimental import pallas as pl
from jax.experimental.pallas import tpu as pltpu

D_MODEL = 4096
SEQ = 8192
DEC_BATCH = 128
PAST_LEN = 2048
PAGE_SIZE = 128
N_PAGES = PAST_LEN // PAGE_SIZE
A_WIDTH = D_MODEL // 2
A_GROUP_WIDTH = 128
A_GROUPS = A_WIDTH // A_GROUP_WIDTH
CHUNK = 128
HEAD_DIM = 128
HEAD_WIDTH = 2 * HEAD_DIM
N_HEADS = D_MODEL // (4 * HEAD_DIM)
B_WIDTH = N_HEADS * HEAD_WIDTH
D_FF = ((8 * D_MODEL // 3 + 255) // 256) * 256
CONV_W = 3
N_IN = 2 * A_WIDTH + 3 * B_WIDTH + 2 * D_MODEL
EPS = 1e-6
NEG_INF = -1e30
LAM_INIT = 0.8 - 0.6 * math.exp(-0.3 * 0)
Q_SCALE = HEAD_DIM ** -0.5
LOG2E = math.log2(math.e)

COL_U = 0
COL_VA = A_WIDTH
COL_Q = 2 * A_WIDTH
COL_K = COL_Q + B_WIDTH
COL_V = COL_K + B_WIDTH
COL_GA = COL_V + B_WIDTH
COL_GB = COL_GA + D_MODEL

V7X_VMEM_LIMIT_BYTES = 58 * 1024 * 1024
FFN_TN = 256
FFN_PANELS = D_FF // FFN_TN
ATT_T = 512
PAGES_PER_STEP = 8

BF16 = jnp.bfloat16
F32 = jnp.float32


def _params(*sem):
    return pltpu.CompilerParams(dimension_semantics=sem, vmem_limit_bytes=V7X_VMEM_LIMIT_BYTES)


def _gelu(x):
    return 0.5 * x * (1.0 + jnp.tanh(0.7978845608028654 * (x + 0.044715 * (x * x * x))))


def _rms(x, g):
    return x * lax.rsqrt(jnp.mean(x * x, axis=-1, keepdims=True) + EPS) * g


M_ALL = SEQ + DEC_BATCH
ROW_T = DEC_BATCH
N_PROMPT_TILES = SEQ // ROW_T

_ALL_ROWS = pl.BlockSpec((ROW_T, D_MODEL), lambda i: (i, 0))
_PROMPT_ROWS = pl.BlockSpec((ROW_T, D_MODEL), lambda i: (jnp.minimum(i, N_PROMPT_TILES - 1), 0))
_SAMPLE_ROWS = pl.BlockSpec((ROW_T, D_MODEL), lambda i: (0, 0))
_VEC = pl.BlockSpec((1, D_MODEL), lambda i: (0, 0))


def _group_rows(xp_ref, xs_ref):
    return jnp.where(pl.program_id(0) < N_PROMPT_TILES, xp_ref[...], xs_ref[...])


def _rms_cast_kernel(xp_ref, xs_ref, g_ref, o_ref):
    o_ref[...] = _rms(_group_rows(xp_ref, xs_ref), g_ref[...]).astype(o_ref.dtype)


def rms_cast(xp, xs, g):
    return pl.pallas_call(
        _rms_cast_kernel, grid=(M_ALL // ROW_T,),
        in_specs=[_PROMPT_ROWS, _SAMPLE_ROWS, _VEC], out_specs=_ALL_ROWS,
        out_shape=jax.ShapeDtypeStruct((M_ALL, D_MODEL), BF16),
        compiler_params=_params("arbitrary"), name="rms_cast")(xp, xs, g.reshape(1, D_MODEL))


def _post_mix_kernel(xp_ref, xs_ref, t_ref, g1_ref, g2_ref, x1_ref, h_ref):
    x1 = _group_rows(xp_ref, xs_ref) + _rms(t_ref[...], g1_ref[...])
    x1_ref[...] = x1
    h_ref[...] = _rms(x1, g2_ref[...]).astype(h_ref.dtype)


def post_mix(xp, xs, t, g_post, g_pre_next):
    return pl.pallas_call(
        _post_mix_kernel, grid=(M_ALL // ROW_T,),
        in_specs=[_PROMPT_ROWS, _SAMPLE_ROWS, _ALL_ROWS, _VEC, _VEC], out_specs=[_ALL_ROWS, _ALL_ROWS],
        out_shape=[jax.ShapeDtypeStruct((M_ALL, D_MODEL), F32), jax.ShapeDtypeStruct((M_ALL, D_MODEL), BF16)],
        compiler_params=_params("arbitrary"), name="post_mix")(
            xp, xs, t, g_post.reshape(1, D_MODEL), g_pre_next.reshape(1, D_MODEL))


def _post_ffn_kernel(x_ref, t_ref, g_ref, yp_ref, ys_ref):
    y = x_ref[...] + _rms(t_ref[...], g_ref[...])
    i = pl.program_id(0)

    @pl.when(i < N_PROMPT_TILES)
    def _():
        yp_ref[...] = y

    @pl.when(i == N_PROMPT_TILES)
    def _():
        ys_ref[...] = y


def post_ffn(x, t, g):
    return pl.pallas_call(
        _post_ffn_kernel, grid=(M_ALL // ROW_T,),
        in_specs=[_ALL_ROWS, _ALL_ROWS, _VEC], out_specs=[_PROMPT_ROWS, _SAMPLE_ROWS],
        out_shape=[jax.ShapeDtypeStruct((SEQ, D_MODEL), F32), jax.ShapeDtypeStruct((DEC_BATCH, D_MODEL), F32)],
        compiler_params=_params("arbitrary"), name="post_ffn")(x, t, g.reshape(1, D_MODEL))


def _mm_kernel(a_ref, w_ref, o_ref):
    o_ref[...] = jnp.dot(a_ref[...], w_ref[...].astype(BF16),
                         preferred_element_type=F32).astype(o_ref.dtype)


def matmul(a, w, tm, tn, name):
    m, k = a.shape
    n = w.shape[1]
    return pl.pallas_call(
        _mm_kernel, grid=(m // tm, n // tn),
        in_specs=[pl.BlockSpec((tm, k), lambda i, j: (i, 0), pipeline_mode=pl.Buffered(1)),
                  pl.BlockSpec((k, tn), lambda i, j: (0, j))],
        out_specs=pl.BlockSpec((tm, tn), lambda i, j: (i, j)),
        out_shape=jax.ShapeDtypeStruct((m, n), F32),
        compiler_params=_params("arbitrary", "arbitrary"), name=name)(a, w)


def _layernorm(x, g, b):
    mu = jnp.mean(x, axis=-1, keepdims=True)
    xc = x - mu
    var = jnp.mean(xc * xc, axis=-1, keepdims=True)
    return xc * lax.rsqrt(var + EPS) * g + b


def _branch_a_prompt_kernel(u_ref, va_ref, lng_ref, lnb_ref, ws_ref, bst_ref, ya_ref):
    u = _gelu(u_ref[...])
    vn = _layernorm(_gelu(va_ref[...]), lng_ref[...], lnb_ref[...])
    row = lax.broadcasted_iota(jnp.int32, (CHUNK, CHUNK), 0)
    col = lax.broadcasted_iota(jnp.int32, (CHUNK, CHUNK), 1)
    causal = row >= col
    for g in range(A_GROUPS):
        sl = slice(g * A_GROUP_WIDTH, (g + 1) * A_GROUP_WIDTH)
        w = jnp.where(causal, ws_ref[g], 0.0).astype(BF16)
        mixed = jnp.dot(w, vn[:, sl].astype(BF16), preferred_element_type=F32) + bst_ref[:, g:g + 1]
        ya_ref[:, sl] = (u[:, sl] * mixed).astype(ya_ref.dtype)


def branch_a_prompt(z, m, ln_g, ln_b, w_s, b_s):
    vec = pl.BlockSpec((1, A_WIDTH), lambda i: (0, 0))
    return pl.pallas_call(
        _branch_a_prompt_kernel, grid=(m // CHUNK,),
        in_specs=[pl.BlockSpec((CHUNK, A_WIDTH), lambda i: (i, COL_U // A_WIDTH)),
                  pl.BlockSpec((CHUNK, A_WIDTH), lambda i: (i, COL_VA // A_WIDTH)),
                  vec, vec,
                  pl.BlockSpec((A_GROUPS, CHUNK, CHUNK), lambda i: (0, 0, 0)),
                  pl.BlockSpec((CHUNK, A_GROUPS), lambda i: (0, 0))],
        out_specs=pl.BlockSpec((CHUNK, A_WIDTH), lambda i: (i, 0)),
        out_shape=jax.ShapeDtypeStruct((m, A_WIDTH), BF16),
        compiler_params=_params("arbitrary"), name="branch_a_prompt")(
            z, z, ln_g.reshape(1, A_WIDTH), ln_b.reshape(1, A_WIDTH), w_s, b_s.T)


def _branch_a_sample_kernel(u_ref, va_ref, lng_ref, lnb_ref, w00_ref, b0_ref, ya_ref, vn_ref):
    u = _gelu(u_ref[...])
    vn = _layernorm(_gelu(va_ref[...]), lng_ref[...], lnb_ref[...])
    vn_ref[...] = vn
    ya_ref[...] = (u * (vn * w00_ref[...] + b0_ref[...])).astype(ya_ref.dtype)


def branch_a_sample(z, row0, m, ln_g, ln_b, w_s, b_s):
    assert row0 % m == 0
    w00 = jnp.repeat(w_s[:, 0, 0], A_GROUP_WIDTH).reshape(1, A_WIDTH)
    b0 = jnp.repeat(b_s[:, 0], A_GROUP_WIDTH).reshape(1, A_WIDTH)
    vec = pl.BlockSpec((1, A_WIDTH), lambda i: (0, 0))
    out = pl.BlockSpec((m, A_WIDTH), lambda i: (0, 0))
    return pl.pallas_call(
        _branch_a_sample_kernel, grid=(1,),
        in_specs=[pl.BlockSpec((m, A_WIDTH), lambda i: (row0 // m, COL_U // A_WIDTH)),
                  pl.BlockSpec((m, A_WIDTH), lambda i: (row0 // m, COL_VA // A_WIDTH)),
                  vec, vec, vec, vec],
        out_specs=[out, out],
        out_shape=[jax.ShapeDtypeStruct((m, A_WIDTH), BF16), jax.ShapeDtypeStruct((m, A_WIDTH), F32)],
        compiler_params=_params("arbitrary"), name="branch_a_sample")(
            z, z, ln_g.reshape(1, A_WIDTH), ln_b.reshape(1, A_WIDTH), w00, b0)


def _lambda(lq1_ref, lk1_ref, lq2_ref, lk2_ref):
    return (jnp.exp(jnp.sum(lq1_ref[...] * lk1_ref[...], axis=1, keepdims=True))
            - jnp.exp(jnp.sum(lq2_ref[...] * lk2_ref[...], axis=1, keepdims=True)) + LAM_INIT)


def _alibi_slopes():
    return np.asarray(2.0 ** (-8.0 * np.arange(1, N_HEADS + 1, dtype=np.float32) / N_HEADS), dtype=np.float32)


N_C_PIECES = 3
EXTRA_COLS = 4 * N_C_PIECES


def _alibi_pieces():
    out = []
    for s in _alibi_slopes().astype(np.float64):
        rest = s * math.log2(math.e)
        for _ in range(N_C_PIECES):
            piece = float(np.asarray(rest, np.float32).astype(BF16).astype(np.float32))
            out.append(piece)
            rest -= piece
    return np.asarray(out, np.float32)


def _split_pos(pos):
    lo = pos % 256
    return lo.astype(F32), (pos - lo).astype(F32)


def _qkv_prep_kernel(c_ref, q_ref, k_ref, v_ref, qt_ref, ka_ref, vt_ref):
    t_tile = ATT_T
    h = pl.program_id(1)
    cs = [c_ref[h * N_C_PIECES + t] for t in range(N_C_PIECES)]

    qt = (q_ref[...] * (Q_SCALE * LOG2E)).T
    row = lax.broadcasted_iota(jnp.int32, (HEAD_DIM, t_tile), 0)
    i_lo, i_hi = _split_pos(lax.broadcasted_iota(jnp.int32, (HEAD_DIM, t_tile), 1))
    qe = jnp.zeros((HEAD_DIM, t_tile), F32)
    for t in range(N_C_PIECES):
        qe = jnp.where(row == 4 * t, i_lo, qe)
        qe = jnp.where(row == 4 * t + 1, i_hi, qe)
        qe = jnp.where((row == 4 * t + 2) | (row == 4 * t + 3), cs[t], qe)
    qt_ref[0] = jnp.concatenate([qt[:HEAD_DIM], qe], axis=0).astype(BF16)
    qt_ref[1] = jnp.concatenate([qt[HEAD_DIM:], qe], axis=0).astype(BF16)

    k = k_ref[...]
    col = lax.broadcasted_iota(jnp.int32, (t_tile, HEAD_DIM), 1)
    j_lo, j_hi = _split_pos(lax.broadcasted_iota(jnp.int32, (t_tile, HEAD_DIM), 0))
    ke = jnp.zeros((t_tile, HEAD_DIM), F32)
    for t in range(N_C_PIECES):
        ke = jnp.where((col == 4 * t) | (col == 4 * t + 1), -cs[t], ke)
        ke = jnp.where(col == 4 * t + 2, j_lo, ke)
        ke = jnp.where(col == 4 * t + 3, j_hi, ke)
    ka_ref[0] = jnp.concatenate([k[:, :HEAD_DIM], ke], axis=1).astype(BF16)
    ka_ref[1] = jnp.concatenate([k[:, HEAD_DIM:], ke], axis=1).astype(BF16)

    vt_ref[...] = v_ref[...].T.astype(BF16)


def qkv_prep(z, seq):
    t_tile = ATT_T
    nt = seq // t_tile
    qb, kb, vb = COL_Q // HEAD_WIDTH, COL_K // HEAD_WIDTH, COL_V // HEAD_WIDTH
    grid_spec = pltpu.PrefetchScalarGridSpec(
        num_scalar_prefetch=1, grid=(nt, N_HEADS),
        in_specs=[pl.BlockSpec((t_tile, HEAD_WIDTH), lambda i, h, c: (i, qb + h)),
                  pl.BlockSpec((t_tile, HEAD_WIDTH), lambda i, h, c: (i, kb + h)),
                  pl.BlockSpec((t_tile, HEAD_WIDTH), lambda i, h, c: (i, vb + h))],
        out_specs=[pl.BlockSpec((None, 2, HEAD_WIDTH, t_tile), lambda i, h, c: (h, 0, 0, i)),
                   pl.BlockSpec((None, 2, None, t_tile, HEAD_WIDTH), lambda i, h, c: (h, 0, i, 0, 0)),
                   pl.BlockSpec((None, None, HEAD_WIDTH, t_tile), lambda i, h, c: (h, i, 0, 0))])
    return pl.pallas_call(
        _qkv_prep_kernel, grid_spec=grid_spec,
        out_shape=[jax.ShapeDtypeStruct((N_HEADS, 2, HEAD_WIDTH, seq), BF16),
                   jax.ShapeDtypeStruct((N_HEADS, 2, nt, t_tile, HEAD_WIDTH), BF16),
                   jax.ShapeDtypeStruct((N_HEADS, nt, HEAD_WIDTH, t_tile), BF16)],
        compiler_params=_params("arbitrary", "arbitrary"), name="qkv_prep")(
            jnp.asarray(_alibi_pieces()), z, z, z)


HEAD_GROUPS = 2


def _attn_pending_chain(n, kv, vt_refs, p_ref, al_ref, acc_ref):
    vt = vt_refs[n // 2][kv]
    acc_ref[n] = acc_ref[n] * al_ref[n] + jnp.dot(vt, p_ref[n], preferred_element_type=F32)


def _attn_scores_chain(n, kj, coff, masked, qt_refs, ka_refs, p_ref, al_ref, m_prev, l_prev):
    t_tile = ATT_T
    g, half = n // 2, n % 2
    s = jnp.dot(ka_refs[g][half, kj], qt_refs[g][half], preferred_element_type=F32)
    if masked:
        key = lax.broadcasted_iota(jnp.int32, (t_tile, t_tile), 0)
        qry = lax.broadcasted_iota(jnp.int32, (t_tile, t_tile), 1)
        s = jnp.where(key > qry, NEG_INF, s)
    m_next = jnp.maximum(m_prev, jnp.max(s, axis=0, keepdims=True) + coff)
    alpha = jnp.exp2(m_prev - m_next)
    p = jnp.exp2(s - (m_next - coff))
    p_ref[n] = p.astype(BF16)
    al_ref[n] = alpha
    return m_next, alpha * l_prev + jnp.sum(p, axis=0, keepdims=True)


SIDE_UNITS = 3
SIDE_GROUPS = 3
SIDE_RING = SIDE_UNITS * SIDE_GROUPS


def _side_copies(unit, pos, pt_ref, ck_ref, cv_ref, kbuf, vbuf, sems):
    page = pt_ref[unit]
    return (pltpu.make_async_copy(ck_ref.at[page], kbuf.at[pos], sems.at[0, pos]),
            pltpu.make_async_copy(cv_ref.at[page], vbuf.at[pos], sems.at[1, pos]))


def _side_scores(unit, pos, state, side):
    (sq_ref, sk_ref, sv_ref, slope_ref, sgr_ref, kbuf, vbuf, m_st, l_st, acc_st, so_ref) = side
    rows = 2 * N_HEADS
    b = unit // N_PAGES
    page_no = unit % N_PAGES
    first = page_no == 0

    q8 = sq_ref[b] * Q_SCALE
    lane = lax.broadcasted_iota(jnp.int32, (N_HEADS, HEAD_WIDTH), 1)
    qmat = jnp.concatenate([jnp.where(lane < HEAD_DIM, q8, 0.0),
                            jnp.where(lane >= HEAD_DIM, q8, 0.0)], axis=0)
    kp = kbuf[pos].astype(BF16)
    s = lax.dot_general(qmat.astype(BF16), kp, (((1,), (1,)), ((), ())), preferred_element_type=F32)
    width = PAGE_SIZE * N_HEADS
    col = lax.broadcasted_iota(jnp.int32, (rows, width), 1)
    row = lax.broadcasted_iota(jnp.int32, (rows, width), 0)
    kpos = page_no * PAGE_SIZE + col // N_HEADS
    bias = -slope_ref[:, :1] * (PAST_LEN - kpos).astype(F32)
    s = jnp.where(col % N_HEADS == row % N_HEADS, s + bias, NEG_INF)

    kn = sk_ref[b]
    vn = sv_ref[b]
    s_new = jnp.sum(qmat * jnp.concatenate([kn, kn], axis=0), axis=1, keepdims=True)
    m_prev = jnp.where(first, jnp.broadcast_to(s_new, m_st.shape), state[0])
    l_prev = jnp.where(first, 1.0, state[1])
    acc_prev = jnp.where(first, jnp.concatenate([vn, vn], axis=0), state[2])

    m_next = jnp.maximum(m_prev, jnp.max(s, axis=1, keepdims=True))
    alpha = jnp.exp(m_prev - m_next)
    p = jnp.exp(s - m_next[:, :1])
    l_next = alpha * l_prev + jnp.sum(p, axis=1, keepdims=True)
    return p.astype(BF16), m_next, l_next, acc_prev * alpha[:, :1]


def _side_values(unit, pos, scored, lam, side):
    (sq_ref, sk_ref, sv_ref, slope_ref, sgr_ref, kbuf, vbuf, m_st, l_st, acc_st, so_ref) = side
    p, m_next, l_next, acc_scaled = scored
    acc = acc_scaled + jnp.dot(p, vbuf[pos].astype(BF16), preferred_element_type=F32)
    o = acc / l_next[:, :1]
    out = o[:N_HEADS] - lam * o[N_HEADS:]
    so_ref[unit // N_PAGES] = _rms(out, sgr_ref[...]) * (1.0 - LAM_INIT)
    return m_next, l_next, acc


def _side_start_slot(slot, n_slots, dma, kbuf, vbuf):
    pt_ref, ck_ref, cv_ref, sems = dma
    src = jnp.minimum(slot, n_slots - 1)
    for k in range(SIDE_UNITS):
        pos = (slot % SIDE_GROUPS) * SIDE_UNITS + k
        for cp in _side_copies(src * SIDE_UNITS + k, pos, pt_ref, ck_ref, cv_ref, kbuf, vbuf, sems):
            cp.start()


def _side_wait_slot(slot, dma, kbuf, vbuf):
    pt_ref, ck_ref, cv_ref, sems = dma
    for k in range(SIDE_UNITS):
        pos = (slot % SIDE_GROUPS) * SIDE_UNITS + k
        for cp in _side_copies(0, pos, pt_ref, ck_ref, cv_ref, kbuf, vbuf, sems):
            cp.wait()


def _attn_slot(g, n_slots, kj, kv_pending, coffs, masked, stats, lam, main, dma, side):
    qt_refs, ka_refs, vt_refs, acc_ref, p_ref, al_ref = main
    pt_ref, ck_ref, cv_ref, sems = dma
    kbuf, vbuf, m_st, l_st, acc_st = side[5], side[6], side[7], side[8], side[9]

    units = [(g * SIDE_UNITS + k, (g % SIDE_GROUPS) * SIDE_UNITS + k) for k in range(SIDE_UNITS)]
    _side_wait_slot(g, dma, kbuf, vbuf)

    n_chain = 2 * HEAD_GROUPS
    pending = [functools.partial(_attn_pending_chain, n, kv_pending, vt_refs, p_ref, al_ref, acc_ref)
               for n in range(n_chain)]
    new_stats = list(stats)

    def scores(n):
        new_stats[2 * n], new_stats[2 * n + 1] = _attn_scores_chain(
            n, kj, coffs[n // 2], masked, qt_refs, ka_refs, p_ref, al_ref, stats[2 * n], stats[2 * n + 1])

    state = (m_st[...], l_st[...], acc_st[...])
    scored = _side_scores(*units[0], state, side)
    pending[0]()
    scores(0)
    state = _side_values(*units[0], scored, lam, side)
    scored = _side_scores(*units[1], state, side)
    pending[1]()
    scores(1)
    state = _side_values(*units[1], scored, lam, side)
    scored = _side_scores(*units[2], state, side)
    pending[2]()
    scores(2)
    pending[3]()
    scores(3)
    state = _side_values(*units[2], scored, lam, side)
    m_st[...], l_st[...], acc_st[...] = state
    _side_start_slot(g + SIDE_GROUPS - 1, n_slots, dma, kbuf, vbuf)
    return tuple(new_stats)


def _attn_prompt_kernel(c_ref, pt_ref,
                        qt_a, qt_b, ka_a, ka_b, vt_a, vt_b, lq1_ref, lk1_ref, lq2_ref, lk2_ref, sg_ref,
                        sgr_ref, sq_ref, sk_ref, sv_ref, slope_ref, ck_ref, cv_ref,
                        o_a, o_b, so_ref,
                        acc_ref, p_ref, al_ref, kbuf, vbuf, sems, m_st, l_st, acc_st):
    assert SIDE_UNITS == 3 and HEAD_GROUPS == 2, "_attn_slot writes its interleaving out for these counts"
    t_tile = ATT_T
    o_refs = (o_a, o_b)
    main = ((qt_a, qt_b), (ka_a, ka_b), (vt_a, vt_b), acc_ref, p_ref, al_ref)
    dma = (pt_ref, ck_ref, cv_ref, sems)
    side = (sq_ref, sk_ref, sv_ref, slope_ref, sgr_ref, kbuf, vbuf, m_st, l_st, acc_st, so_ref)
    nt = ka_a.shape[1]
    slots_per_pair = nt * (nt + 1) // 2
    n_slots = pl.num_programs(0) * slots_per_pair
    hp = pl.program_id(0)
    qi = pl.program_id(1)
    g_base = hp * slots_per_pair + (qi * (qi + 1)) // 2
    c_sums = []
    for g in range(HEAD_GROUPS):
        base = (hp + g * (N_HEADS // HEAD_GROUPS)) * N_C_PIECES
        c_sum = c_ref[base]
        for t in range(1, N_C_PIECES):
            c_sum = c_sum + c_ref[base + t]
        c_sums.append(c_sum)
    lam = _lambda(lq1_ref, lk1_ref, lq2_ref, lk2_ref)

    @pl.when((hp == 0) & (qi == 0))
    def _():
        so_ref[...] = jnp.zeros(so_ref.shape, F32)
        m_st[...] = jnp.zeros(m_st.shape, F32)
        l_st[...] = jnp.zeros(l_st.shape, F32)
        acc_st[...] = jnp.zeros(acc_st.shape, F32)
        for slot in range(SIDE_GROUPS - 1):
            _side_start_slot(slot, n_slots, dma, kbuf, vbuf)

    acc_ref[...] = jnp.zeros(acc_ref.shape, F32)
    p_ref[...] = jnp.zeros(p_ref.shape, BF16)
    al_ref[...] = jnp.ones(al_ref.shape, F32)
    neg = jnp.full((1, t_tile), NEG_INF, F32)
    zero = jnp.zeros((1, t_tile), F32)

    def body(kj, stats):
        dist = ((qi - kj) * t_tile).astype(F32)
        return _attn_slot(g_base + kj, n_slots, kj, jnp.maximum(kj - 1, 0), [-c * dist for c in c_sums], False,
                          stats, lam, main, dma, side)

    stats = lax.fori_loop(0, qi, body, (neg, zero) * (2 * HEAD_GROUPS))
    stats = _attn_slot(g_base + qi, n_slots, qi, jnp.maximum(qi - 1, 0), [0.0] * HEAD_GROUPS, True,
                       stats, lam, main, dma, side)
    for n in range(2 * HEAD_GROUPS):
        _attn_pending_chain(n, qi, main[2], p_ref, al_ref, acc_ref)

    @pl.when((hp == pl.num_programs(0) - 1) & (qi == pl.num_programs(1) - 1))
    def _():
        for extra in range(SIDE_GROUPS - 1):
            _side_wait_slot(n_slots + extra, dma, kbuf, vbuf)

    for g in range(HEAD_GROUPS):
        l1, l2 = stats[4 * g + 1], stats[4 * g + 3]
        out = acc_ref[2 * g] / l1 - lam * (acc_ref[2 * g + 1] / l2)
        ms = jnp.mean(out * out, axis=0, keepdims=True)
        y = out * lax.rsqrt(ms + EPS) * sg_ref[...] * (1.0 - LAM_INIT)
        o_refs[g][...] = y.T.astype(o_refs[g].dtype)


def _sample_heads(zs, c0):
    return zs[:, c0:c0 + B_WIDTH].reshape(zs.shape[0], N_HEADS, HEAD_WIDTH)


def _slope_rows():
    return jnp.asarray(np.broadcast_to(np.tile(_alibi_slopes(), 2)[:, None], (2 * N_HEADS, 128)).copy())


def side_sequences(seq):
    nt = seq // ATT_T
    n_units = SIDE_UNITS * (N_HEADS // HEAD_GROUPS) * (nt * (nt + 1) // 2)
    assert n_units % N_PAGES == 0, "the side stream must end on a sequence boundary"
    return n_units // N_PAGES


def attn_prompt(z, seq, zs, cache_k, cache_v, page_table, lq1, lk1, lq2, lk2, subln_g):
    t_tile = ATT_T
    nt = seq // t_tile
    hg = N_HEADS // HEAD_GROUPS
    nb = zs.shape[0]
    assert side_sequences(seq) <= nb
    n_pool = cache_k.shape[0]
    page_rows = PAGE_SIZE * N_HEADS
    ck = cache_k.reshape(n_pool, page_rows, HEAD_WIDTH)
    cv = cache_v.reshape(n_pool, page_rows, HEAD_WIDTH)
    qt, ka, vt = qkv_prep(z, seq)
    const = lambda shape: pl.BlockSpec(shape, lambda h, i, c, pt: (0,) * len(shape))
    whole = lambda shape: pl.BlockSpec(shape, lambda h, i, c, pt: (0,) * len(shape), pipeline_mode=pl.Buffered(1))
    lvec = const((1, HEAD_DIM))
    qspec = lambda g: pl.BlockSpec((None, 2, HEAD_WIDTH, t_tile), lambda h, i, c, pt: (h + g * hg, 0, 0, i))
    kspec = lambda g: pl.BlockSpec((None, 2, nt, t_tile, HEAD_WIDTH), lambda h, i, c, pt: (h + g * hg, 0, 0, 0, 0),
                                   pipeline_mode=pl.Buffered(1))
    vspec = lambda g: pl.BlockSpec((None, nt, HEAD_WIDTH, t_tile), lambda h, i, c, pt: (h + g * hg, 0, 0, 0),
                                   pipeline_mode=pl.Buffered(1))
    ospec = pl.BlockSpec((t_tile, HEAD_WIDTH), lambda h, i, c, pt: (i, h))
    heads = (nb, N_HEADS, HEAD_WIDTH)
    n_chain = 2 * HEAD_GROUPS
    grid_spec = pltpu.PrefetchScalarGridSpec(
        num_scalar_prefetch=2, grid=(hg, nt),
        in_specs=[qspec(0), qspec(1), kspec(0), kspec(1), vspec(0), vspec(1),
                  lvec, lvec, lvec, lvec, const((HEAD_WIDTH, 1)), const((1, HEAD_WIDTH)),
                  whole(heads), whole(heads), whole(heads), const((2 * N_HEADS, 128)),
                  pl.BlockSpec(memory_space=pl.ANY), pl.BlockSpec(memory_space=pl.ANY)],
        out_specs=[ospec, ospec, const(heads)],
        scratch_shapes=[pltpu.VMEM((n_chain, HEAD_WIDTH, t_tile), F32),
                        pltpu.VMEM((n_chain, t_tile, t_tile), BF16),
                        pltpu.VMEM((n_chain, 1, t_tile), F32),
                        pltpu.VMEM((SIDE_RING, page_rows, HEAD_WIDTH), F32),
                        pltpu.VMEM((SIDE_RING, page_rows, HEAD_WIDTH), F32),
                        pltpu.SemaphoreType.DMA((2, SIDE_RING)),
                        pltpu.VMEM((2 * N_HEADS, 128), F32),
                        pltpu.VMEM((2 * N_HEADS, 128), F32),
                        pltpu.VMEM((2 * N_HEADS, HEAD_WIDTH), F32)])
    r = lambda a: a.reshape(1, -1)
    half_out = jax.ShapeDtypeStruct((seq, B_WIDTH // HEAD_GROUPS), BF16)
    y_a, y_b, y_s = pl.pallas_call(
        _attn_prompt_kernel, grid_spec=grid_spec,
        out_shape=[half_out, half_out, jax.ShapeDtypeStruct(heads, F32)],
        compiler_params=_params("arbitrary", "arbitrary"), name="attn_prompt")(
            jnp.asarray(_alibi_pieces()), page_table.reshape(-1),
            qt, qt, ka, ka, vt, vt, r(lq1), r(lk1), r(lq2), r(lk2),
            subln_g.reshape(HEAD_WIDTH, 1), r(subln_g),
            _sample_heads(zs, COL_Q), _sample_heads(zs, COL_K), _sample_heads(zs, COL_V), _slope_rows(), ck, cv)
    return jnp.concatenate([y_a, y_b], axis=1), y_s


def _attn_sample_kernel(pt_ref, q_ref, kn_ref, vn_ref, slope_ref, lq1_ref, lk1_ref, lq2_ref, lk2_ref, sg_ref,
                        *rest):
    npg = PAGES_PER_STEP
    k_refs = rest[:npg]
    v_refs = rest[npg:2 * npg]
    o_ref, qmat_ref, m_ref, l_ref, acc_ref = rest[2 * npg:]
    s_id = pl.program_id(1)
    rows = 2 * N_HEADS

    @pl.when(s_id == 0)
    def _():
        q8 = q_ref[...] * Q_SCALE
        lane = lax.broadcasted_iota(jnp.int32, (N_HEADS, HEAD_WIDTH), 1)
        qmat = jnp.concatenate([jnp.where(lane < HEAD_DIM, q8, 0.0),
                                jnp.where(lane >= HEAD_DIM, q8, 0.0)], axis=0)
        qmat_ref[...] = qmat.astype(BF16)
        kn = kn_ref[...]
        vn = vn_ref[...]
        s_new = jnp.sum(qmat * jnp.concatenate([kn, kn], axis=0), axis=1, keepdims=True)
        m_ref[...] = jnp.broadcast_to(s_new, m_ref.shape)
        l_ref[...] = jnp.ones(l_ref.shape, F32)
        acc_ref[...] = jnp.concatenate([vn, vn], axis=0)

    qmat = qmat_ref[...]
    scores = []
    for i in range(npg):
        kp = k_refs[i][...].astype(BF16)
        scores.append(lax.dot_general(qmat, kp, (((1,), (1,)), ((), ())), preferred_element_type=F32))
    width = npg * PAGE_SIZE * N_HEADS
    col = lax.broadcasted_iota(jnp.int32, (rows, width), 1)
    row = lax.broadcasted_iota(jnp.int32, (rows, width), 0)
    pos = s_id * (npg * PAGE_SIZE) + col // N_HEADS
    bias = -slope_ref[:, :1] * (PAST_LEN - pos).astype(F32)
    s = jnp.where(col % N_HEADS == row % N_HEADS, jnp.concatenate(scores, axis=1) + bias, NEG_INF)
    m_prev = m_ref[...]
    m_next = jnp.maximum(m_prev, jnp.max(s, axis=1, keepdims=True))
    alpha = jnp.exp(m_prev - m_next)
    p = jnp.exp(s - m_next[:, :1])
    l_ref[...] = alpha * l_ref[...] + jnp.sum(p, axis=1, keepdims=True)
    m_ref[...] = m_next
    acc = acc_ref[...] * alpha[:, :1]
    pw = PAGE_SIZE * N_HEADS
    for i in range(npg):
        acc = acc + jnp.dot(p[:, i * pw:(i + 1) * pw].astype(BF16),
                            v_refs[i][...].astype(BF16), preferred_element_type=F32)
    acc_ref[...] = acc

    @pl.when(s_id == pl.num_programs(1) - 1)
    def _():
        lam = _lambda(lq1_ref, lk1_ref, lq2_ref, lk2_ref)
        o = acc_ref[...] / l_ref[...][:, :1]
        out = o[:N_HEADS] - lam * o[N_HEADS:]
        o_ref[...] = _rms(out, sg_ref[...]) * (1.0 - LAM_INIT)


def attn_sample(z, b0, cache_k, cache_v, page_table, lq1, lk1, lq2, lk2, subln_g):
    nb = z.shape[0] - b0
    npg = PAGES_PER_STEP
    n_steps = N_PAGES // npg
    n_pool = cache_k.shape[0]
    ck = cache_k.reshape(n_pool, PAGE_SIZE * N_HEADS, HEAD_WIDTH)
    cv = cache_v.reshape(n_pool, PAGE_SIZE * N_HEADS, HEAD_WIDTH)
    inblk = pl.BlockSpec((None, N_HEADS, HEAD_WIDTH), lambda b, s, pt: (b0 + b, 0, 0))
    outblk = pl.BlockSpec((None, N_HEADS, HEAD_WIDTH), lambda b, s, pt: (b, 0, 0))
    lvec = pl.BlockSpec((1, HEAD_DIM), lambda b, s, pt: (0, 0))

    def page_spec(i):
        return pl.BlockSpec((None, PAGE_SIZE * N_HEADS, HEAD_WIDTH),
                            lambda b, s, pt: (pt[(b0 + b) * N_PAGES + s * npg + i], 0, 0))

    grid_spec = pltpu.PrefetchScalarGridSpec(
        num_scalar_prefetch=1, grid=(nb, n_steps),
        in_specs=[inblk, inblk, inblk,
                  pl.BlockSpec((2 * N_HEADS, 128), lambda b, s, pt: (0, 0)),
                  lvec, lvec, lvec, lvec,
                  pl.BlockSpec((1, HEAD_WIDTH), lambda b, s, pt: (0, 0))]
                 + [page_spec(i) for i in range(npg)] + [page_spec(i) for i in range(npg)],
        out_specs=outblk,
        scratch_shapes=[pltpu.VMEM((2 * N_HEADS, HEAD_WIDTH), BF16),
                        pltpu.VMEM((2 * N_HEADS, 128), F32),
                        pltpu.VMEM((2 * N_HEADS, 128), F32),
                        pltpu.VMEM((2 * N_HEADS, HEAD_WIDTH), F32)])
    r = lambda a: a.reshape(1, -1)
    return pl.pallas_call(
        _attn_sample_kernel, grid_spec=grid_spec,
        out_shape=jax.ShapeDtypeStruct((nb, N_HEADS, HEAD_WIDTH), F32),
        compiler_params=_params("arbitrary", "arbitrary"), name="attn_sample")(
            page_table.reshape(-1), _sample_heads(z, COL_Q), _sample_heads(z, COL_K), _sample_heads(z, COL_V),
            _slope_rows(), r(lq1), r(lk1), r(lq2), r(lk2), r(subln_g), *([ck] * npg), *([cv] * npg))


def _merge_kernel(ya_ref, yb_ref, wa_ref, wb_ref, ga_ref, gb_ref, o_ref):
    a = jnp.dot(ya_ref[...], wa_ref[...].astype(BF16), preferred_element_type=F32)
    b = jnp.dot(yb_ref[...], wb_ref[...].astype(BF16), preferred_element_type=F32)
    o_ref[...] = (jax.nn.sigmoid(ga_ref[...]) * a + jax.nn.sigmoid(gb_ref[...]) * b).astype(o_ref.dtype)


def merge(ya, yb, z, w_a, w_b, tm, tn):
    m = ya.shape[0]
    act = lambda: pl.BlockSpec((tm, A_WIDTH), lambda i, j: (i, 0), pipeline_mode=pl.Buffered(1))
    wsp = pl.BlockSpec((A_WIDTH, tn), lambda i, j: (0, j))
    return pl.pallas_call(
        _merge_kernel, grid=(m // tm, D_MODEL // tn),
        in_specs=[act(), act(), wsp, wsp,
                  pl.BlockSpec((tm, tn), lambda i, j: (i, COL_GA // tn + j)),
                  pl.BlockSpec((tm, tn), lambda i, j: (i, COL_GB // tn + j))],
        out_specs=pl.BlockSpec((tm, tn), lambda i, j: (i, j)),
        out_shape=jax.ShapeDtypeStruct((m, D_MODEL), BF16),
        compiler_params=_params("arbitrary", "arbitrary"), name="merge")(ya, yb, w_a, w_b, z, z)


def _conv3(up, prev1, prev2, cw_ref, cb_ref, own_state=None):
    rows = lax.broadcasted_iota(jnp.int32, up.shape, 0)
    s1 = jnp.where(rows == 0, prev1, pltpu.roll(up, 1, axis=0))
    s2 = jnp.where(rows == 0, prev2, jnp.where(rows == 1, prev1, pltpu.roll(up, 2, axis=0)))
    if own_state is not None:
        mask, st1, st0 = own_state
        s1 = jnp.where(mask, st1, s1)
        s2 = jnp.where(mask, st0, s2)
    return cb_ref[...] + cw_ref[0:1, :] * s2 + cw_ref[1:2, :] * s1 + cw_ref[2:3, :] * up


def _ffn_up_kernel(h_ref, wg_ref, wv_ref, cwg_ref, cwv_ref, cbg_ref, cbv_ref,
                   s0g_ref, s0v_ref, s1g_ref, s1v_ref,
                   act_ref, tailg_ref, tailv_ref, upsg_ref, upsv_ref, carry_ref, *, subs):
    i = pl.program_id(0)
    j = pl.program_id(1)
    n_samp = s0g_ref.shape[0]
    n_sub = len(subs)
    edge = subs[-1] - n_samp
    wg = wg_ref[...].astype(BF16)
    wv = wv_ref[...].astype(BF16)

    @pl.when(i == 0)
    def _():
        carry_ref[j] = jnp.zeros(carry_ref.shape[1:], F32)

    prev = carry_ref[j]
    pg1, pg2 = prev[7:8, :FFN_TN], prev[6:7, :FFN_TN]
    pv1, pv2 = prev[7:8, FFN_TN:], prev[6:7, FFN_TN:]
    for r, sub in enumerate(subs):
        rows = slice(sum(subs[:r]), sum(subs[:r + 1]))
        h = h_ref[rows, :]
        upg = jnp.dot(h, wg, preferred_element_type=F32)
        upv = jnp.dot(h, wv, preferred_element_type=F32)
        own_g = own_v = None
        if r == n_sub - 1:
            local = lax.broadcasted_iota(jnp.int32, (sub, FFN_TN), 0)
            mask = (local >= edge) & (i == pl.num_programs(0) - 1)
            pad = lambda ref: jnp.concatenate([jnp.zeros((edge, FFN_TN), F32), ref[...]], axis=0)
            own_g = (mask, pad(s1g_ref), pad(s0g_ref))
            own_v = (mask, pad(s1v_ref), pad(s0v_ref))
        cg = _conv3(upg, pg1, pg2, cwg_ref, cbg_ref, own_g)
        cv = _conv3(upv, pv1, pv2, cwv_ref, cbv_ref, own_v)
        act_ref[rows, :] = (_gelu(cg) * cv).astype(act_ref.dtype)
        pg1, pg2 = upg[sub - 1:sub, :], upg[sub - 2:sub - 1, :]
        pv1, pv2 = upv[sub - 1:sub, :], upv[sub - 2:sub - 1, :]
    carry_ref[j] = jnp.concatenate([upg[sub - 8:, :], upv[sub - 8:, :]], axis=1)
    tailg_ref[...] = upg[edge - 8:edge, :]
    tailv_ref[...] = upv[edge - 8:edge, :]
    upsg_ref[...] = upg[edge:, :]
    upsv_ref[...] = upv[edge:, :]


def ffn_up(h, w_up, conv_w, conv_b, state, subs):
    m = h.shape[0]
    tm = sum(subs)
    n_mt = m // tm
    n_samp = state.shape[0]
    assert m % tm == 0 and subs[-1] >= n_samp + 8 and all(s % 16 == 0 for s in subs)
    np_ = FFN_PANELS
    cb = conv_b.reshape(1, 2 * D_FF)
    st = state.reshape(n_samp, (CONV_W - 1) * 2 * D_FF)
    sblk = lambda off: pl.BlockSpec((n_samp, FFN_TN), lambda i, j: (0, off + j))
    tail = pl.BlockSpec((None, 8, FFN_TN), lambda i, j: (i, 0, j))
    ups = pl.BlockSpec((None, n_samp, FFN_TN), lambda i, j: (i, 0, j))
    act, tail_g, tail_v, ups_g, ups_v = pl.pallas_call(
        functools.partial(_ffn_up_kernel, subs=tuple(subs)), grid=(n_mt, np_),
        in_specs=[pl.BlockSpec((tm, D_MODEL), lambda i, j: (i, 0), pipeline_mode=pl.Buffered(1)),
                  pl.BlockSpec((D_MODEL, FFN_TN), lambda i, j: (0, j)),
                  pl.BlockSpec((D_MODEL, FFN_TN), lambda i, j: (0, np_ + j)),
                  pl.BlockSpec((CONV_W, FFN_TN), lambda i, j: (0, j)),
                  pl.BlockSpec((CONV_W, FFN_TN), lambda i, j: (0, np_ + j)),
                  pl.BlockSpec((1, FFN_TN), lambda i, j: (0, j)),
                  pl.BlockSpec((1, FFN_TN), lambda i, j: (0, np_ + j)),
                  sblk(0), sblk(np_), sblk(2 * np_), sblk(3 * np_)],
        out_specs=[pl.BlockSpec((tm, FFN_TN), lambda i, j: (i, j)), tail, tail, ups, ups],
        out_shape=[jax.ShapeDtypeStruct((m, D_FF), BF16),
                   jax.ShapeDtypeStruct((n_mt, 8, D_FF), F32),
                   jax.ShapeDtypeStruct((n_mt, 8, D_FF), F32),
                   jax.ShapeDtypeStruct((n_mt, n_samp, D_FF), F32),
                   jax.ShapeDtypeStruct((n_mt, n_samp, D_FF), F32)],
        scratch_shapes=[pltpu.VMEM((np_, 8, 2 * FFN_TN), F32)],
        compiler_params=_params("arbitrary", "arbitrary"), name="ffn_up")(
            h, w_up, w_up, conv_w, conv_w, cb, cb, st, st, st, st)
    keep = 8 - (CONV_W - 1)
    conv_prompt = jnp.concatenate([tail_g[n_mt - 1, keep:], tail_v[n_mt - 1, keep:]], axis=-1)
    up_s = jnp.concatenate([ups_g[n_mt - 1], ups_v[n_mt - 1]], axis=-1)
    conv_sample = jnp.concatenate([state[:, 1:], up_s[:, None, :]], axis=1)
    return act, conv_prompt, conv_sample


TM_BIG = M_ALL // 4
TM_DOWN = M_ALL // 8
FFN_SUBS = (TM_BIG // 5,) * 5
assert sum(FFN_SUBS) == TM_BIG


def kernel(x_prompt, x_sample, cache_k, cache_v, state_ffn_conv, page_table, g_pre_mix, w_in, a_ln_g, a_ln_b, a_w_s, a_b_s, lambda_q1, lambda_k1, lambda_q2, lambda_k2, subln_g, w_branch_a, w_branch_b, w_out, g_post_mix, g_pre_ffn, w_up, conv_w, conv_b, w_down, g_post_ffn):
    l = 0
    p = dict(g_pre_mix=g_pre_mix[l], w_in=w_in[l], a_ln_g=a_ln_g[l], a_ln_b=a_ln_b[l], a_w_s=a_w_s[l],
             a_b_s=a_b_s[l], w_branch_a=w_branch_a[l], w_branch_b=w_branch_b[l], w_out=w_out[l],
             g_post_mix=g_post_mix[l], g_pre_ffn=g_pre_ffn[l], w_up=w_up[l], conv_w=conv_w[l],
             conv_b=conv_b[l], w_down=w_down[l], g_post_ffn=g_post_ffn[l])
    lams = (lambda_q1[l], lambda_k1[l], lambda_q2[l], lambda_k2[l], subln_g[l])
    gate_args = (p["a_ln_g"], p["a_ln_b"], p["a_w_s"], p["a_b_s"])

    xp = x_prompt.reshape(SEQ, D_MODEL)
    xs = x_sample.reshape(DEC_BATCH, D_MODEL)
    h = rms_cast(xp, xs, p["g_pre_mix"])
    z = matmul(h, p["w_in"], TM_BIG, 512, "in_proj")
    zs = z[SEQ:]

    n_side = side_sequences(SEQ)
    yb_p, yb_side = attn_prompt(z, SEQ, zs, cache_k[l], cache_v[l], page_table, *lams)
    yb_rest = attn_sample(zs, n_side, cache_k[l], cache_v[l], page_table, *lams)
    yb_s = jnp.concatenate([yb_side[:n_side], yb_rest], axis=0).reshape(DEC_BATCH, B_WIDTH).astype(BF16)
    ya_p = branch_a_prompt(z, SEQ, *gate_args)
    ya_s, vn_s = branch_a_sample(z, SEQ, DEC_BATCH, *gate_args)

    mixed = merge(jnp.concatenate([ya_p, ya_s], axis=0), jnp.concatenate([yb_p, yb_s], axis=0), z,
                  p["w_branch_a"], p["w_branch_b"], TM_BIG, 256)
    t = matmul(mixed, p["w_out"], TM_BIG, 512, "out_proj")
    x1, h2 = post_mix(xp, xs, t, p["g_post_mix"], p["g_pre_ffn"])
    act, conv_p, conv_s = ffn_up(h2, p["w_up"], p["conv_w"], p["conv_b"], state_ffn_conv[l], FFN_SUBS)
    t2 = matmul(act, p["w_down"], TM_DOWN, 256, "down_proj")
    yp, ys = post_ffn(x1, t2, p["g_post_ffn"])

    hs = (N_HEADS, HEAD_WIDTH)
    return (yp.reshape(1, SEQ, D_MODEL),
            ys.reshape(DEC_BATCH, 1, D_MODEL),
            z[:SEQ, COL_K:COL_V].reshape(1, 1, SEQ, *hs),
            z[:SEQ, COL_V:COL_GA].reshape(1, 1, SEQ, *hs),
            zs[:, COL_K:COL_V].reshape(1, DEC_BATCH, 1, *hs),
            zs[:, COL_V:COL_GA].reshape(1, DEC_BATCH, 1, *hs),
            vn_s.reshape(1, DEC_BATCH, 1, A_WIDTH),
            conv_p.reshape(1, 1, CONV_W - 1, 2 * D_FF),
            conv_s.reshape(1, DEC_BATCH, CONV_W - 1, 2 * D_FF))
```
